```python
import jax, jax.numpy as jnp
from jax import lax
import numpy as np

D_MODEL = 1024
BATCH = 8
SEQ = 2048
DEPTH = 1
DEC_BATCH = 128
DEC_SEQ = 8
PAST_LEN = 16384
PAGE_SIZE = 128

D_PLE = 256
CONF_WIDTH = D_MODEL // 2
CONF_K = 31
DN_HEADS = 4
DN_HEAD_DIM = 128
DN_WIDTH = DN_HEADS * DN_HEAD_DIM
DN_CONV_K = 4
MIX_WIDTH = CONF_WIDTH + DN_WIDTH
IN_COLS = 2 * CONF_WIDTH + 4 * DN_WIDTH + 2 * DN_HEADS
D_FF = 2816
FFN_CONV_K = 3
CHUNK = 64
EPS = 1e-6

kernel_name = 'hybrid_conformer_gdn_step'


def _rmsnorm(x, g):
    xf = x.astype(jnp.float32)
    y = xf * lax.rsqrt(jnp.mean(jnp.square(xf), axis=-1, keepdims=True) + EPS) * g.astype(jnp.float32)
    return y.astype(x.dtype)


def _l2norm(x):
    return x * lax.rsqrt(jnp.sum(jnp.square(x), axis=-1, keepdims=True) + EPS)


def _causal_dwconv(x, buf, w, b=None):
    K, C = w.shape
    xp = jnp.concatenate([buf.astype(x.dtype), x], axis=1)
    y = lax.conv_general_dilated(xp, w[:, None, :].astype(x.dtype), window_strides=(1,), padding='VALID',
                                 dimension_numbers=('NWC', 'WIO', 'NWC'), feature_group_count=C)
    if b is not None:
        y = y + b.astype(y.dtype)
    return y, xp[:, xp.shape[1] - (K - 1):]


def _split_cols(u):
    outs, o = [], 0
    for wdt in (CONF_WIDTH, CONF_WIDTH, 3 * DN_WIDTH, DN_WIDTH, DN_HEADS, DN_HEADS):
        outs.append(u[..., o:o + wdt])
        o += wdt
    return outs


def _conformer_conv(val, gate, buf, w, b, ln_g, ln_b):
    glu = val * jax.nn.sigmoid(gate)
    c, new_buf = _causal_dwconv(glu, buf, w, b)
    cf = c.astype(jnp.float32)
    mu = jnp.mean(cf, axis=-1, keepdims=True)
    var = jnp.mean(jnp.square(cf - mu), axis=-1, keepdims=True)
    n = (cf - mu) * lax.rsqrt(var + EPS) * ln_g.astype(jnp.float32) + ln_b.astype(jnp.float32)
    return jax.nn.silu(n).astype(val.dtype), new_buf


def _gated_delta_chunked(q, k, v, g, beta, S0):
    Bn, L, H, Dk = q.shape
    Dv = v.shape[-1]
    pad = (-L) % CHUNK
    if pad:
        pw = ((0, 0), (0, pad), (0, 0), (0, 0))
        q, k, v = jnp.pad(q, pw), jnp.pad(k, pw), jnp.pad(v, pw)
        g, beta = jnp.pad(g, pw[:3]), jnp.pad(beta, pw[:3])
    N = (L + pad) // CHUNK

    def blk(t):
        t = t.reshape((Bn, N, CHUNK) + t.shape[2:])
        return jnp.moveaxis(jnp.moveaxis(t, 1, 0), 2, 3)

    qc = blk(q) * (Dk ** -0.5)
    kc, vc, bc = blk(k), blk(v), blk(beta)
    gc = jnp.cumsum(blk(g), axis=-1)
    idx = jnp.arange(CHUNK)
    causal = idx[:, None] >= idx[None, :]
    strict = idx[:, None] > idx[None, :]
    decay = jnp.exp(jnp.where(causal, gc[..., :, None] - gc[..., None, :], -jnp.inf))
    kb = kc * bc[..., None]
    a_low = jnp.where(strict, jnp.einsum('nbhid,nbhjd->nbhij', kb, kc) * decay, 0.0)
    rhs = jnp.concatenate([vc * bc[..., None], kb * jnp.exp(gc)[..., None]], axis=-1)
    sol = lax.linalg.triangular_solve(a_low, rhs, left_side=True, lower=True, unit_diagonal=True)
    u, w = sol[..., :Dv], sol[..., Dv:]
    qk = jnp.where(causal, jnp.einsum('nbhid,nbhjd->nbhij', qc, kc) * decay, 0.0)
    q_dec = qc * jnp.exp(gc)[..., None]
    k_dec = kc * jnp.exp(gc[..., -1:] - gc)[..., None]
    g_tot = jnp.exp(gc[..., -1])[..., None, None]

    def step(S, xs):
        u_i, w_i, qk_i, qd_i, kd_i, gt_i = xs
        v_new = u_i - jnp.einsum('bhck,bhkv->bhcv', w_i, S)
        o_i = jnp.einsum('bhck,bhkv->bhcv', qd_i, S) + jnp.einsum('bhij,bhjv->bhiv', qk_i, v_new)
        S = S * gt_i + jnp.einsum('bhck,bhcv->bhkv', kd_i, v_new)
        return S, o_i

    S_fin, o = lax.scan(step, S0, (u, w, qk, q_dec, k_dec, g_tot))
    o = jnp.moveaxis(jnp.moveaxis(o, 3, 2), 0, 1).reshape(Bn, N * CHUNK, H, Dv)[:, :L]
    return o, S_fin


def _layer(x, p, conf_buf, dn_buf, S, ffn_buf, norm_mix_g, w_in, conf_dw_w, conf_dw_b, conf_ln_g, conf_ln_b,
           dn_conv_w, dn_a_log, dn_dt_bias, dn_norm_g, w_out, norm_ffn_g, w_up, ffn_conv_w, ffn_conv_b,
           w_down, w_ple, w_ple_gate):
    Bn, L, _ = x.shape
    h = _rmsnorm(x, norm_mix_g)
    u = h @ w_in
    c_val, c_gate, qkv, z, b_in, a_in = _split_cols(u)
    conf_out, conf_buf_new = _conformer_conv(c_val, c_gate, conf_buf, conf_dw_w, conf_dw_b, conf_ln_g, conf_ln_b)
    qkv_c, dn_buf_new = _causal_dwconv(qkv, dn_buf, dn_conv_w)
    qkv_c = jax.nn.silu(qkv_c.astype(jnp.float32))
    q, k, v = [t.reshape(Bn, L, DN_HEADS, DN_HEAD_DIM) for t in jnp.split(qkv_c, 3, axis=-1)]
    q, k = _l2norm(q), _l2norm(k)
    beta = jax.nn.sigmoid(b_in.astype(jnp.float32))
    g = -jnp.exp(dn_a_log.astype(jnp.float32)) * jax.nn.softplus(a_in.astype(jnp.float32) + dn_dt_bias.astype(jnp.float32))
    o, S_new = _gated_delta_chunked(q, k, v, g, beta, S.astype(jnp.float32))
    o = _rmsnorm(o, dn_norm_g) * jax.nn.silu(z.astype(jnp.float32).reshape(Bn, L, DN_HEADS, DN_HEAD_DIM))
    o = o.reshape(Bn, L, DN_WIDTH).astype(x.dtype)
    x = x + jnp.concatenate([conf_out, o], axis=-1) @ w_out
    up, ffn_buf_new = _causal_dwconv(_rmsnorm(x, norm_ffn_g) @ w_up, ffn_buf, ffn_conv_w, ffn_conv_b)
    gate, val = jnp.split(up, 2, axis=-1)
    x = x + (jax.nn.silu(gate) * val) @ w_down
    x = x + (p @ w_ple) * jax.nn.sigmoid(x @ w_ple_gate)
    return x, conf_buf_new, dn_buf_new, S_new.astype(S.dtype), ffn_buf_new


def _trunk(x, p, conf_buf, dn_buf, S, ffn_buf, layer_w, norm_final_g):
    outs = []
    for i in range(DEPTH):
        x, cb, db, s, fb = _layer(x, p[i], conf_buf[i], dn_buf[i], S[i], ffn_buf[i], *[wt[i] for wt in layer_w])
        outs.append((cb, db, s, fb))
    new = [jnp.stack([o[j] for o in outs]) for j in range(4)]
    return _rmsnorm(x, norm_final_g), new[0], new[1], new[2], new[3]


def setup_inputs(seed: int = 0) -> dict:
    key = jax.random.key(seed)
    ks = jax.random.split(key, 32)
    nrm = lambda k, shape, s=1.0: jax.random.normal(k, shape, jnp.float32) * s
    gain = lambda k, shape: 1.0 + 0.02 * jax.random.normal(k, shape, jnp.float32)
    dt = jnp.exp(jax.random.uniform(ks[20], (DEPTH, DN_HEADS), jnp.float32, np.log(1e-3), np.log(1e-1)))
    return {
        'x_prompt': nrm(ks[0], (BATCH, SEQ, D_MODEL)),
        'x_sample': nrm(ks[1], (DEC_BATCH, DEC_SEQ, D_MODEL)),
        'p_prompt': nrm(ks[2], (DEPTH, BATCH, SEQ, D_PLE)),
        'p_sample': nrm(ks[3], (DEPTH, DEC_BATCH, DEC_SEQ, D_PLE)),
        'state_conf_buf': nrm(ks[4], (DEPTH, DEC_BATCH, CONF_K - 1, CONF_WIDTH), 0.5),
        'state_dn_conv_buf': nrm(ks[5], (DEPTH, DEC_BATCH, DN_CONV_K - 1, 3 * DN_WIDTH)),
        'state_dn_S': nrm(ks[6], (DEPTH, DEC_BATCH, DN_HEADS, DN_HEAD_DIM, DN_HEAD_DIM)),
        'state_ffn_buf': nrm(ks[7], (DEPTH, DEC_BATCH, FFN_CONV_K - 1, 2 * D_FF)),
        'norm_mix_g': gain(ks[8], (DEPTH, D_MODEL)),
        'w_in': nrm(ks[9], (DEPTH, D_MODEL, IN_COLS), D_MODEL ** -0.5),
        'conf_dw_w': nrm(ks[10], (DEPTH, CONF_K, CONF_WIDTH), CONF_K ** -0.5),
        'conf_dw_b': nrm(ks[11], (DEPTH, CONF_WIDTH), 0.01),
        'conf_ln_g': gain(ks[12], (DEPTH, CONF_WIDTH)),
        'conf_ln_b': nrm(ks[13], (DEPTH, CONF_WIDTH), 0.02),
        'dn_conv_w': nrm(ks[14], (DEPTH, DN_CONV_K, 3 * DN_WIDTH), DN_CONV_K ** -0.5),
        'dn_a_log': jnp.log(jax.random.uniform(ks[15], (DEPTH, DN_HEADS), jnp.float32, 1.0, 16.0)),
        'dn_dt_bias': dt + jnp.log(-jnp.expm1(-dt)),
        'dn_norm_g': gain(ks[16], (DEPTH, DN_HEAD_DIM)),
        'w_out': nrm(ks[17], (DEPTH, MIX_WIDTH, D_MODEL), MIX_WIDTH ** -0.5),
        'norm_ffn_g': gain(ks[18], (DEPTH, D_MODEL)),
        'w_up': nrm(ks[19], (DEPTH, D_MODEL, 2 * D_FF), D_MODEL ** -0.5),
        'ffn_conv_w': nrm(ks[21], (DEPTH, FFN_CONV_K, 2 * D_FF), FFN_CONV_K ** -0.5),
        'ffn_conv_b': nrm(ks[22], (DEPTH, 2 * D_FF), 0.01),
        'w_down': nrm(ks[23], (DEPTH, D_FF, D_MODEL), D_FF ** -0.5),
        'w_ple': nrm(ks[24], (DEPTH, D_PLE, D_MODEL), D_PLE ** -0.5),
        'w_ple_gate': nrm(ks[25], (DEPTH, D_MODEL, D_MODEL), D_MODEL ** -0.5),
        'norm_final_g': gain(ks[26], (D_MODEL,)),
    }


def reference(x_prompt, x_sample, p_prompt, p_sample, state_conf_buf, state_dn_conv_buf, state_dn_S, state_ffn_buf,
              norm_mix_g, w_in, conf_dw_w, conf_dw_b, conf_ln_g, conf_ln_b, dn_conv_w, dn_a_log, dn_dt_bias,
              dn_norm_g, w_out, norm_ffn_g, w_up, ffn_conv_w, ffn_conv_b, w_down, w_ple, w_ple_gate, norm_final_g):
    layer_w = (norm_mix_g, w_in, conf_dw_w, conf_dw_b, conf_ln_g, conf_ln_b, dn_conv_w, dn_a_log, dn_dt_bias,
               dn_norm_g, w_out, norm_ffn_g, w_up, ffn_conv_w, ffn_conv_b, w_down, w_ple, w_ple_gate)
    Bp, dt = x_prompt.shape[0], x_prompt.dtype
    z_conf = jnp.zeros((DEPTH, Bp, CONF_K - 1, CONF_WIDTH), dt)
    z_dn = jnp.zeros((DEPTH, Bp, DN_CONV_K - 1, 3 * DN_WIDTH), dt)
    z_S = jnp.zeros((DEPTH, Bp, DN_HEADS, DN_HEAD_DIM, DN_HEAD_DIM), dt)
    z_ffn = jnp.zeros((DEPTH, Bp, FFN_CONV_K - 1, 2 * D_FF), dt)
    y_prompt, pc, pd, ps, pf = _trunk(x_prompt, p_prompt, z_conf, z_dn, z_S, z_ffn, layer_w, norm_final_g)
    y_sample, sc, sd, ss, sf = _trunk(x_sample, p_sample, state_conf_buf, state_dn_conv_buf, state_dn_S,
                                      state_ffn_buf, layer_w, norm_final_g)
    return (y_prompt, y_sample, pc, pd, ps, pf, sc, sd, ss, sf)
```

```python
import functools

import jax
import jax.numpy as jnp
from jax import lax
from jax.experimental import pallas as pl
from jax.experimental.pallas import tpu as pltpu

F32 = jnp.float32
BF16 = jnp.bfloat16

D_MODEL = 1024
D_PLE = 256
CONF_WIDTH = 512
CONF_K = 31
DN_HEADS = 4
DN_HEAD_DIM = 128
DN_WIDTH = DN_HEADS * DN_HEAD_DIM
DN_CONV_K = 4
D_FF = 2816
FFN_CONV_K = 3
CHUNK = 64
EPS = 1e-6

SUBLANES = 8
LANES = 128
FF_COLS = 256
N_FF_CHUNKS = D_FF // FF_COLS
SAMPLE_RS = 32
VMEM_LIMIT = 56 * 1024 * 1024


def _round_up(n, m):
    return (n + m - 1) // m * m


def _mm(a, b):
    return jnp.dot(a.astype(BF16), b.astype(BF16), preferred_element_type=F32)


def _mm_nt(a, b):
    return lax.dot_general(a.astype(BF16), b.astype(BF16), (((1,), (1,)), ((), ())),
                           preferred_element_type=F32)


def _mm_tn(a, b):
    return lax.dot_general(a.astype(BF16), b.astype(BF16), (((0,), (0,)), ((), ())),
                           preferred_element_type=F32)


def _split3(x):
    hi = x.astype(BF16)
    r1 = x - hi.astype(F32)
    mid = r1.astype(BF16)
    lo = (r1 - mid.astype(F32)).astype(BF16)
    return hi, mid, lo


def _mm_exact_lhs(m_bf16, x):
    hi, mid, lo = _split3(x)
    dot = functools.partial(jnp.dot, preferred_element_type=F32)
    return dot(m_bf16, hi) + dot(m_bf16, mid) + dot(m_bf16, lo)


def _sigmoid(x):
    return 1.0 / (1.0 + jnp.exp(-x))


def _silu(x):
    return x * _sigmoid(x)


def _softplus(x):
    return jnp.maximum(x, 0.0) + jnp.log1p(jnp.exp(-jnp.abs(x)))


def _rms(x, g):
    return x * lax.rsqrt(jnp.mean(x * x, axis=-1, keepdims=True) + EPS) * g


def _premix_kernel(x_ref, g_ref, win_ref, wba_ref, cw_ref, cb_ref, lng_ref, lnb_ref, dw_ref,
                   alog_ref, dtb_ref, cbuf0_ref, dbuf0_ref,
                   conf_ref, qkv_ref, z_ref, bg_ref, cbuf_ref, dbuf_ref,
                   cext, dext, *, R, rs, NT):
    t = pl.program_id(1)
    c_halo = (CONF_K - 1) * rs
    d_halo = (DN_CONV_K - 1) * rs
    HC = _round_up(c_halo, SUBLANES)
    HD = _round_up(d_halo, SUBLANES)

    @pl.when(t == 0)
    def _():
        cext[HC - c_halo:HC, :] = cbuf0_ref[...]
        dext[HD - d_halo:HD, :] = dbuf0_ref[...]

    if NT > 1:
        @pl.when(t > 0)
        def _():
            cext[0:HC, :] = cext[R:R + HC, :]
            dext[0:HD, :] = dext[R:R + HD, :]

    h = _rms(x_ref[...], g_ref[...]).astype(BF16)
    vg = jnp.dot(h, win_ref[:, 0:2 * CONF_WIDTH], preferred_element_type=F32)
    cext[HC:HC + R, :] = vg[:, :CONF_WIDTH] * _sigmoid(vg[:, CONF_WIDTH:])
    dext[HD:HD + R, :] = jnp.dot(h, win_ref[:, 2 * CONF_WIDTH:2 * CONF_WIDTH + 3 * DN_WIDTH],
                                 preferred_element_type=F32)
    z_ref[...] = jnp.dot(h, win_ref[:, 2 * CONF_WIDTH + 3 * DN_WIDTH:], preferred_element_type=F32)

    ba = jnp.dot(h, wba_ref[...], preferred_element_type=F32)
    lane = lax.broadcasted_iota(jnp.int32, ba.shape, 1)
    gate = -jnp.exp(alog_ref[...]) * _softplus(ba + dtb_ref[...])
    bg_ref[...] = jnp.where(lane < DN_HEADS, _sigmoid(ba), gate)

    RC = 32
    for r0 in range(0, R, RC):
        acc = jnp.broadcast_to(cb_ref[...], (RC, CONF_WIDTH))
        for k in range(CONF_K):
            off = HC - (CONF_K - 1 - k) * rs + r0
            acc = acc + cw_ref[k:k + 1, :] * cext[off:off + RC, :]
        mu = jnp.mean(acc, axis=-1, keepdims=True)
        cen = acc - mu
        var = jnp.mean(cen * cen, axis=-1, keepdims=True)
        n = cen * lax.rsqrt(var + EPS) * lng_ref[...] + lnb_ref[...]
        conf_ref[r0:r0 + RC, :] = _silu(n)

    RD = 16
    for r0 in range(0, R, RD):
        acc = None
        for k in range(DN_CONV_K):
            off = HD - (DN_CONV_K - 1 - k) * rs + r0
            term = dw_ref[k:k + 1, :] * dext[off:off + RD, :]
            acc = term if acc is None else acc + term
        y = _silu(acc)
        for j in range(3 * DN_HEADS):
            seg = y[:, j * DN_HEAD_DIM:(j + 1) * DN_HEAD_DIM]
            if j < 2 * DN_HEADS:
                seg = seg * lax.rsqrt(jnp.sum(seg * seg, axis=-1, keepdims=True) + EPS)
            qkv_ref[r0:r0 + RD, j * DN_HEAD_DIM:(j + 1) * DN_HEAD_DIM] = seg

    @pl.when(t == NT - 1)
    def _():
        cbuf_ref[...] = cext[HC + R - c_halo:HC + R, :]
        dbuf_ref[...] = dext[HD + R - d_halo:HD + R, :]


def _premix(x, cbuf0, dbuf0, g, w_in_main, w_ba, cw, cb, lng, lnb, dw, alog, dtb, *, R, rs):
    G, L, _ = x.shape
    NT = L // R
    c_halo = (CONF_K - 1) * rs
    d_halo = (DN_CONV_K - 1) * rs
    HC = _round_up(c_halo, SUBLANES)
    HD = _round_up(d_halo, SUBLANES)

    def tile(c):
        return pl.BlockSpec((None, R, c), lambda gi, ti: (gi, ti, 0))

    def full(a):
        return pl.BlockSpec(a.shape, lambda gi, ti: (0,) * a.ndim)

    def per_group(rows, c):
        return pl.BlockSpec((None, rows, c), lambda gi, ti: (gi, 0, 0))

    weights = (g, w_in_main, w_ba, cw, cb, lng, lnb, dw, alog, dtb)
    return pl.pallas_call(
        functools.partial(_premix_kernel, R=R, rs=rs, NT=NT),
        grid=(G, NT),
        in_specs=[tile(D_MODEL)] + [full(a) for a in weights]
        + [per_group(c_halo, CONF_WIDTH), per_group(d_halo, 3 * DN_WIDTH)],
        out_specs=[tile(CONF_WIDTH), tile(3 * DN_WIDTH), tile(DN_WIDTH), tile(LANES),
                   per_group(c_halo, CONF_WIDTH), per_group(d_halo, 3 * DN_WIDTH)],
        out_shape=[jax.ShapeDtypeStruct((G, L, CONF_WIDTH), F32),
                   jax.ShapeDtypeStruct((G, L, 3 * DN_WIDTH), F32),
                   jax.ShapeDtypeStruct((G, L, DN_WIDTH), F32),
                   jax.ShapeDtypeStruct((G, L, LANES), F32),
                   jax.ShapeDtypeStruct((G, c_halo, CONF_WIDTH), F32),
                   jax.ShapeDtypeStruct((G, d_halo, 3 * DN_WIDTH), F32)],
        scratch_shapes=[pltpu.VMEM((HC + R, CONF_WIDTH), F32), pltpu.VMEM((HD + R, 3 * DN_WIDTH), F32)],
        compiler_params=pltpu.CompilerParams(dimension_semantics=("arbitrary", "arbitrary"),
                                             vmem_limit_bytes=VMEM_LIMIT),
        name="premix",
    )(x, *weights, cbuf0, dbuf0)


def _neumann(a, eye, squarings):
    x = eye - a
    p = a
    for _ in range(squarings):
        p = _mm(p, p)
        x = x + _mm(x, p)
    return x


def _unit_lower_inverse_64(a, ri, ci):
    eye = (ri == ci).astype(F32)
    same_block = lax.shift_right_logical(ri, 4) == lax.shift_right_logical(ci, 4)
    a_diag = jnp.where(same_block, a, 0.0)
    a_off = jnp.where(same_block, 0.0, a)
    d = _neumann(a_diag, eye, 3)
    b = _mm(d, a_off)
    return _mm(_neumann(b, eye, 1), d)


def _head_norm_gate(o, z, ng):
    return o * lax.rsqrt(jnp.mean(o * o, axis=-1, keepdims=True) + EPS) * ng * _silu(z)


def _delta_prompt_kernel(qkv_ref, z_ref, bg_ref, ng_ref, s0_ref, o_ref, snew_ref, s_scr, *, R, NT):
    t = pl.program_id(1)

    @pl.when(t == 0)
    def _():
        s_scr[...] = s0_ref[...]

    bg = bg_ref[...]
    ri = lax.broadcasted_iota(jnp.int32, (R, R), 0)
    ci = lax.broadcasted_iota(jnp.int32, (R, R), 1)
    same_chunk = lax.shift_right_logical(ri, 6) == lax.shift_right_logical(ci, 6)
    csum = jnp.where(same_chunk & (ci <= ri), 1.0, 0.0).astype(BF16)
    gc_all = _mm_exact_lhs(csum, bg)
    gc_t = gc_all.T

    ri64 = lax.broadcasted_iota(jnp.int32, (CHUNK, CHUNK), 0)
    ci64 = lax.broadcasted_iota(jnp.int32, (CHUNK, CHUNK), 1)
    causal = ri64 >= ci64
    strict = ri64 > ci64
    scale = DN_HEAD_DIM ** -0.5

    for c in range(R // CHUNK):
        r0 = c * CHUNK
        for h in range(DN_HEADS):
            lo = h * DN_HEAD_DIM
            q = qkv_ref[r0:r0 + CHUNK, lo:lo + DN_HEAD_DIM] * scale
            k = qkv_ref[r0:r0 + CHUNK, DN_WIDTH + lo:DN_WIDTH + lo + DN_HEAD_DIM]
            v = qkv_ref[r0:r0 + CHUNK, 2 * DN_WIDTH + lo:2 * DN_WIDTH + lo + DN_HEAD_DIM]
            beta = bg[r0:r0 + CHUNK, h:h + 1]
            gcc = gc_all[r0:r0 + CHUNK, DN_HEADS + h:DN_HEADS + h + 1]
            gcr = gc_t[DN_HEADS + h:DN_HEADS + h + 1, r0:r0 + CHUNK]
            glast = gc_all[r0 + CHUNK - 1:r0 + CHUNK, DN_HEADS + h:DN_HEADS + h + 1]
            decay = jnp.exp(jnp.where(causal, gcc - gcr, -jnp.inf))
            kb = k * beta
            a = jnp.where(strict, _mm_nt(kb, k) * decay, 0.0)
            tinv = _unit_lower_inverse_64(a, ri64, ci64)
            eg = jnp.exp(gcc)
            uw = _mm(tinv, jnp.concatenate([v * beta, kb * eg], axis=1))
            u = uw[:, :DN_HEAD_DIM]
            w = uw[:, DN_HEAD_DIM:]
            qk = jnp.where(causal, _mm_nt(q, k) * decay, 0.0)
            kd = k * jnp.exp(glast - gcc)
            s = s_scr[h]
            v_new = u - _mm(w, s)
            o = _mm(q * eg, s) + _mm(qk, v_new)
            s_scr[h] = s * jnp.exp(glast) + _mm_tn(kd, v_new)
            o_ref[r0:r0 + CHUNK, lo:lo + DN_HEAD_DIM] = _head_norm_gate(
                o, z_ref[r0:r0 + CHUNK, lo:lo + DN_HEAD_DIM], ng_ref[...])

    @pl.when(t == NT - 1)
    def _():
        snew_ref[...] = s_scr[...]


def _delta_prompt(qkv, z, bg, ng, s0, *, R):
    B, L, _ = qkv.shape
    NT = L // R

    def tile(c):
        return pl.BlockSpec((None, R, c), lambda bi, ti: (bi, ti, 0))

    s_spec = pl.BlockSpec((None, DN_HEADS, DN_HEAD_DIM, DN_HEAD_DIM), lambda bi, ti: (bi, 0, 0, 0))
    return pl.pallas_call(
        functools.partial(_delta_prompt_kernel, R=R, NT=NT),
        grid=(B, NT),
        in_specs=[tile(3 * DN_WIDTH), tile(DN_WIDTH), tile(LANES),
                  pl.BlockSpec(ng.shape, lambda bi, ti: (0, 0)), s_spec],
        out_specs=[tile(DN_WIDTH), s_spec],
        out_shape=[jax.ShapeDtypeStruct((B, L, DN_WIDTH), F32), jax.ShapeDtypeStruct(s0.shape, F32)],
        scratch_shapes=[pltpu.VMEM((DN_HEADS, DN_HEAD_DIM, DN_HEAD_DIM), F32)],
        compiler_params=pltpu.CompilerParams(dimension_semantics=("arbitrary", "arbitrary"),
                                             vmem_limit_bytes=VMEM_LIMIT),
        name="delta_prompt",
    )(qkv, z, bg, ng, s0)


def _delta_sample_kernel(qkv_ref, z_ref, bg_ref, ng_ref, s0_ref, o_ref, snew_ref,
                         w_scr, qd_scr, kd_scr, u_scr, vn_scr, os_scr, *, T, rs):
    R = T * rs
    shift = rs.bit_length() - 1
    bg = bg_ref[...]
    ri = lax.broadcasted_iota(jnp.int32, (R, R), 0)
    ci = lax.broadcasted_iota(jnp.int32, (R, R), 1)
    same_seq = (ri & (rs - 1)) == (ci & (rs - 1))
    ti = lax.shift_right_logical(ri, shift)
    tj = lax.shift_right_logical(ci, shift)
    causal = same_seq & (ti >= tj)
    strict = same_seq & (ti > tj)
    eye = (ri == ci).astype(F32)
    gc_all = _mm_exact_lhs(jnp.where(causal, 1.0, 0.0).astype(BF16), bg)
    gc_t = gc_all.T
    scale = DN_HEAD_DIM ** -0.5
    squarings = (T - 1).bit_length() - 1

    for h in range(DN_HEADS):
        lo = h * DN_HEAD_DIM
        q = qkv_ref[:, lo:lo + DN_HEAD_DIM] * scale
        k = qkv_ref[:, DN_WIDTH + lo:DN_WIDTH + lo + DN_HEAD_DIM]
        v = qkv_ref[:, 2 * DN_WIDTH + lo:2 * DN_WIDTH + lo + DN_HEAD_DIM]
        beta = bg[:, h:h + 1]
        gcc = gc_all[:, DN_HEADS + h:DN_HEADS + h + 1]
        gcr = gc_t[DN_HEADS + h:DN_HEADS + h + 1, :]
        glast = gc_all[R - rs:R, DN_HEADS + h:DN_HEADS + h + 1]
        glast_rows = jnp.concatenate([glast] * T, axis=0)
        decay = jnp.exp(jnp.where(causal, gcc - gcr, -jnp.inf))
        kb = k * beta
        a = jnp.where(strict, _mm_nt(kb, k) * decay, 0.0)
        tinv = _neumann(a, eye, squarings)
        eg = jnp.exp(gcc)
        uw = _mm(tinv, jnp.concatenate([v * beta, kb * eg], axis=1))
        u_scr[...] = uw[:, :DN_HEAD_DIM]
        w_scr[...] = uw[:, DN_HEAD_DIM:]
        qd_scr[...] = q * eg
        kd_scr[...] = k * jnp.exp(glast_rows - gcc)
        qk = jnp.where(causal, _mm_nt(q, k) * decay, 0.0)
        eglast = jnp.exp(glast)
        for s in range(rs):
            rows = pl.ds(s, T, stride=rs)
            state = s0_ref[s, h]
            ws = _mm(jnp.concatenate([w_scr[rows, :], qd_scr[rows, :]], axis=0), state)
            v_new = u_scr[rows, :] - ws[:T]
            vn_scr[rows, :] = v_new
            os_scr[rows, :] = ws[T:]
            snew_ref[s, h] = state * eglast[s:s + 1, :] + _mm_tn(kd_scr[rows, :], v_new)
        o = os_scr[...] + _mm(qk, vn_scr[...])
        o_ref[:, lo:lo + DN_HEAD_DIM] = _head_norm_gate(o, z_ref[:, lo:lo + DN_HEAD_DIM], ng_ref[...])


def _delta_sample(qkv, z, bg, ng, s0, *, T, rs):
    G, R, _ = qkv.shape

    def tile(c):
        return pl.BlockSpec((None, R, c), lambda gi: (gi, 0, 0))

    s_spec = pl.BlockSpec((rs, DN_HEADS, DN_HEAD_DIM, DN_HEAD_DIM), lambda gi: (gi, 0, 0, 0))
    return pl.pallas_call(
        functools.partial(_delta_sample_kernel, T=T, rs=rs),
        grid=(G,),
        in_specs=[tile(3 * DN_WIDTH), tile(DN_WIDTH), tile(LANES),
                  pl.BlockSpec(ng.shape, lambda gi: (0, 0)), s_spec],
        out_specs=[tile(DN_WIDTH), s_spec],
        out_shape=[jax.ShapeDtypeStruct((G, R, DN_WIDTH), F32), jax.ShapeDtypeStruct(s0.shape, F32)],
        scratch_shapes=[pltpu.VMEM((R, DN_HEAD_DIM), F32) for _ in range(6)],
        compiler_params=pltpu.CompilerParams(dimension_semantics=("arbitrary",),
                                             vmem_limit_bytes=VMEM_LIMIT),
        name="delta_sample",
    )(qkv, z, bg, ng, s0)


def _postmix_kernel(x_ref, conf_ref, o_ref, p_ref, fbuf0_ref, wout_ref, gffn_ref, wup_ref, fw_ref, fb_ref,
                    wdown_ref, wple_ref, wpg_ref, gfin_ref,
                    y_ref, fbuf_ref, fext, act_scr, *, R, rs, NT):
    t = pl.program_id(1)
    halo = (FFN_CONV_K - 1) * rs
    HF = _round_up(halo, SUBLANES)

    @pl.when(t == 0)
    def _():
        fext[HF - halo:HF, :] = fbuf0_ref[...]

    if NT > 1:
        @pl.when(t > 0)
        def _():
            fext[0:HF, :] = fext[R:R + HF, :]

    mix = jnp.concatenate([conf_ref[...].astype(BF16), o_ref[...].astype(BF16)], axis=1)
    x1 = x_ref[...] + jnp.dot(mix, wout_ref[...], preferred_element_type=F32)
    hn = _rms(x1, gffn_ref[...]).astype(BF16)
    for j in range(2 * N_FF_CHUNKS):
        c0 = j * FF_COLS
        fext[HF:HF + R, c0:c0 + FF_COLS] = jnp.dot(hn, wup_ref[:, c0:c0 + FF_COLS],
                                                   preferred_element_type=F32)

    RC = 64
    for j in range(N_FF_CHUNKS):
        for r0 in range(0, R, RC):
            halves = []
            for c0 in (j * FF_COLS, D_FF + j * FF_COLS):
                acc = jnp.broadcast_to(fb_ref[:, c0:c0 + FF_COLS], (RC, FF_COLS))
                for k in range(FFN_CONV_K):
                    off = HF - (FFN_CONV_K - 1 - k) * rs + r0
                    acc = acc + fw_ref[k:k + 1, c0:c0 + FF_COLS] * fext[off:off + RC, c0:c0 + FF_COLS]
                halves.append(acc)
            act_scr[r0:r0 + RC, j * FF_COLS:(j + 1) * FF_COLS] = (_silu(halves[0]) * halves[1]).astype(BF16)

    x2 = x1 + jnp.dot(act_scr[...], wdown_ref[...], preferred_element_type=F32)
    ple = jnp.dot(p_ref[...].astype(BF16), wple_ref[...], preferred_element_type=F32)
    x3 = x2 + ple * _sigmoid(jnp.dot(x2.astype(BF16), wpg_ref[...], preferred_element_type=F32))
    y_ref[...] = _rms(x3, gfin_ref[...])

    @pl.when(t == NT - 1)
    def _():
        fbuf_ref[...] = fext[HF + R - halo:HF + R, :]


def _postmix(x, conf, o, p, fbuf0, w_out, g_ffn, w_up, fw, fb, w_down, w_ple, w_pg, g_fin, *, R, rs):
    G, L, _ = x.shape
    NT = L // R
    halo = (FFN_CONV_K - 1) * rs
    HF = _round_up(halo, SUBLANES)

    def tile(c):
        return pl.BlockSpec((None, R, c), lambda gi, ti: (gi, ti, 0))

    def full(a):
        return pl.BlockSpec(a.shape, lambda gi, ti: (0,) * a.ndim)

    buf_spec = pl.BlockSpec((None, halo, 2 * D_FF), lambda gi, ti: (gi, 0, 0))
    weights = (w_out, g_ffn, w_up, fw, fb, w_down, w_ple, w_pg, g_fin)
    return pl.pallas_call(
        functools.partial(_postmix_kernel, R=R, rs=rs, NT=NT),
        grid=(G, NT),
        in_specs=[tile(D_MODEL), tile(CONF_WIDTH), tile(DN_WIDTH), tile(D_PLE), buf_spec]
        + [full(a) for a in weights],
        out_specs=[tile(D_MODEL), buf_spec],
        out_shape=[jax.ShapeDtypeStruct((G, L, D_MODEL), F32),
                   jax.ShapeDtypeStruct((G, halo, 2 * D_FF), F32)],
        scratch_shapes=[pltpu.VMEM((HF + R, 2 * D_FF), F32), pltpu.VMEM((R, D_FF), BF16)],
        compiler_params=pltpu.CompilerParams(dimension_semantics=("arbitrary", "arbitrary"),
                                             vmem_limit_bytes=VMEM_LIMIT),
        name="postmix",
    )(x, conf, o, p, fbuf0, *weights)


def _interleave(a, rs):
    n, t, c = a.shape
    return a.reshape(n // rs, rs, t, c).transpose(0, 2, 1, 3).reshape(n // rs, t * rs, c)


def _deinterleave(a, rs, t):
    g, _, c = a.shape
    return a.reshape(g, t, rs, c).transpose(0, 2, 1, 3).reshape(g * rs, t, c)


def _trunk(x, p, conf_buf, dn_buf, s0, ffn_buf, w, *, interleaved):
    n, seq, _ = x.shape
    if interleaved:
        rs = SAMPLE_RS
        x, p, conf_buf, dn_buf, ffn_buf = (_interleave(a, rs) for a in (x, p, conf_buf, dn_buf, ffn_buf))
        R = seq * rs
    else:
        rs = 1
        R = 512
    conf, qkv, z, bg, conf_new, dn_new = _premix(
        x, conf_buf, dn_buf, w["g_mix"], w["w_in_main"], w["w_ba"], w["conf_w"], w["conf_b"],
        w["ln_g"], w["ln_b"], w["dn_w"], w["a_log"], w["dt_bias"], R=R, rs=rs)
    if interleaved:
        o, s_new = _delta_sample(qkv, z, bg, w["dn_norm_g"], s0, T=seq, rs=rs)
    else:
        o, s_new = _delta_prompt(qkv, z, bg, w["dn_norm_g"], s0, R=256)
    y, ffn_new = _postmix(x, conf, o, p, ffn_buf, w["w_out"], w["g_ffn"], w["w_up"], w["ffn_w"], w["ffn_b"],
                          w["w_down"], w["w_ple"], w["w_pg"], w["g_fin"], R=min(R, 256), rs=rs)
    if interleaved:
        y = _deinterleave(y, rs, seq)
        conf_new = _deinterleave(conf_new, rs, CONF_K - 1)
        dn_new = _deinterleave(dn_new, rs, DN_CONV_K - 1)
        ffn_new = _deinterleave(ffn_new, rs, FFN_CONV_K - 1)
    return y, conf_new[None], dn_new[None], s_new[None], ffn_new[None]


def kernel(x_prompt, x_sample, p_prompt, p_sample, state_conf_buf, state_dn_conv_buf, state_dn_S, state_ffn_buf,
           norm_mix_g, w_in, conf_dw_w, conf_dw_b, conf_ln_g, conf_ln_b, dn_conv_w, dn_a_log, dn_dt_bias,
           dn_norm_g, w_out, norm_ffn_g, w_up, ffn_conv_w, ffn_conv_b, w_down, w_ple, w_ple_gate, norm_final_g):
    assert w_in.shape[0] == 1, "single trunk layer"
    n_main = 2 * CONF_WIDTH + 4 * DN_WIDTH
    gate_pad = ((0, 0), (DN_HEADS, LANES - 2 * DN_HEADS))
    w = dict(
        g_mix=norm_mix_g[0][None], w_in_main=w_in[0][:, :n_main].astype(BF16),
        w_ba=jnp.pad(w_in[0][:, n_main:], ((0, 0), (0, LANES - 2 * DN_HEADS))).astype(BF16),
        conf_w=conf_dw_w[0], conf_b=conf_dw_b[0][None], ln_g=conf_ln_g[0][None], ln_b=conf_ln_b[0][None],
        dn_w=dn_conv_w[0], a_log=jnp.pad(dn_a_log[0][None], gate_pad), dt_bias=jnp.pad(dn_dt_bias[0][None], gate_pad),
        dn_norm_g=dn_norm_g[0][None], w_out=w_out[0].astype(BF16), g_ffn=norm_ffn_g[0][None],
        w_up=w_up[0].astype(BF16), ffn_w=ffn_conv_w[0], ffn_b=ffn_conv_b[0][None], w_down=w_down[0].astype(BF16),
        w_ple=w_ple[0].astype(BF16), w_pg=w_ple_gate[0].astype(BF16), g_fin=norm_final_g[None],
    )
    bp, dt = x_prompt.shape[0], x_prompt.dtype
    z_conf = jnp.zeros((bp, CONF_K - 1, CONF_WIDTH), dt)
    z_dn = jnp.zeros((bp, DN_CONV_K - 1, 3 * DN_WIDTH), dt)
    z_s = jnp.zeros((bp, DN_HEADS, DN_HEAD_DIM, DN_HEAD_DIM), dt)
    z_ffn = jnp.zeros((bp, FFN_CONV_K - 1, 2 * D_FF), dt)
    yp, pc, pd, ps, pf = _trunk(x_prompt, p_prompt[0], z_conf, z_dn, z_s, z_ffn, w, interleaved=False)
    ys, sc, sd, ss, sf = _trunk(x_sample, p_sample[0], state_conf_buf[0], state_dn_conv_buf[0], state_dn_S[0],
                                state_ffn_buf[0], w, interleaved=True)
    return (yp, ys, pc, pd, ps, pf, sc, sd, ss, sf)
```

```python
import functools

import jax
import jax.numpy as jnp
from jax import lax
from jax.experimental import pallas as pl
from jax.experimental.pallas import tpu as pltpu

F32 = jnp.float32
BF16 = jnp.bfloat16

D_MODEL = 1024
D_PLE = 256
CONF_WIDTH = 512
CONF_K = 31
DN_HEADS = 4
DN_HEAD_DIM = 128
DN_WIDTH = DN_HEADS * DN_HEAD_DIM
DN_CONV_K = 4
D_FF = 2816
FFN_CONV_K = 3
CHUNK = 64
EPS = 1e-6

SUBLANES = 8
LANES = 128
FF_COLS = 256
N_FF_CHUNKS = D_FF // FF_COLS
SAMPLE_RS = 32
VMEM_LIMIT = 56 * 1024 * 1024


def _round_up(n, m):
    return (n + m - 1) // m * m


def _mm(a, b):
    return jnp.dot(a.astype(BF16), b.astype(BF16), preferred_element_type=F32)


def _mm_nt(a, b):
    return lax.dot_general(a.astype(BF16), b.astype(BF16), (((1,), (1,)), ((), ())),
                           preferred_element_type=F32)


def _mm_tn(a, b):
    return lax.dot_general(a.astype(BF16), b.astype(BF16), (((0,), (0,)), ((), ())),
                           preferred_element_type=F32)


def _split3(x):
    hi = x.astype(BF16)
    r1 = x - hi.astype(F32)
    mid = r1.astype(BF16)
    lo = (r1 - mid.astype(F32)).astype(BF16)
    return hi, mid, lo


def _mm_exact_lhs(m_bf16, x):
    hi, mid, lo = _split3(x)
    dot = functools.partial(jnp.dot, preferred_element_type=F32)
    return dot(m_bf16, hi) + dot(m_bf16, mid) + dot(m_bf16, lo)


def _sigmoid(x):
    return 1.0 / (1.0 + jnp.exp(-x))


def _silu(x):
    return x * _sigmoid(x)


def _softplus(x):
    return jnp.maximum(x, 0.0) + jnp.log1p(jnp.exp(-jnp.abs(x)))


def _rms(x, g):
    return x * lax.rsqrt(jnp.mean(x * x, axis=-1, keepdims=True) + EPS) * g


def _premix_kernel(x_ref, g_ref, win_ref, wba_ref, cw_ref, cb_ref, lng_ref, lnb_ref, dw_ref,
                   alog_ref, dtb_ref, cbuf0_ref, dbuf0_ref,
                   conf_ref, qkv_ref, z_ref, bg_ref, cbuf_ref, dbuf_ref,
                   cext, dext, *, R, rs, NT):
    t = pl.program_id(1)
    c_halo = (CONF_K - 1) * rs
    d_halo = (DN_CONV_K - 1) * rs
    HC = _round_up(c_halo, SUBLANES)
    HD = _round_up(d_halo, SUBLANES)

    @pl.when(t == 0)
    def _():
        cext[HC - c_halo:HC, :] = cbuf0_ref[...]
        dext[HD - d_halo:HD, :] = dbuf0_ref[...]

    if NT > 1:
        @pl.when(t > 0)
        def _():
            cext[0:HC, :] = cext[R:R + HC, :]
            dext[0:HD, :] = dext[R:R + HD, :]

    h = _rms(x_ref[...], g_ref[...]).astype(BF16)
    vg = jnp.dot(h, win_ref[:, 0:2 * CONF_WIDTH], preferred_element_type=F32)
    cext[HC:HC + R, :] = vg[:, :CONF_WIDTH] * _sigmoid(vg[:, CONF_WIDTH:])
    dext[HD:HD + R, :] = jnp.dot(h, win_ref[:, 2 * CONF_WIDTH:2 * CONF_WIDTH + 3 * DN_WIDTH],
                                 preferred_element_type=F32)
    z_ref[...] = jnp.dot(h, win_ref[:, 2 * CONF_WIDTH + 3 * DN_WIDTH:], preferred_element_type=F32)

    ba = jnp.dot(h, wba_ref[...], preferred_element_type=F32)
    lane = lax.broadcasted_iota(jnp.int32, ba.shape, 1)
    gate = -jnp.exp(alog_ref[...]) * _softplus(ba + dtb_ref[...])
    bg_ref[...] = jnp.where(lane < DN_HEADS, _sigmoid(ba), gate)

    RC = 32
    for r0 in range(0, R, RC):
        acc = jnp.broadcast_to(cb_ref[...], (RC, CONF_WIDTH))
        for k in range(CONF_K):
            off = HC - (CONF_K - 1 - k) * rs + r0
            acc = acc + cw_ref[k:k + 1, :] * cext[off:off + RC, :]
        mu = jnp.mean(acc, axis=-1, keepdims=True)
        cen = acc - mu
        var = jnp.mean(cen * cen, axis=-1, keepdims=True)
        n = cen * lax.rsqrt(var + EPS) * lng_ref[...] + lnb_ref[...]
        conf_ref[r0:r0 + RC, :] = _silu(n)

    RD = 16
    for r0 in range(0, R, RD):
        acc = None
        for k in range(DN_CONV_K):
            off = HD - (DN_CONV_K - 1 - k) * rs + r0
            term = dw_ref[k:k + 1, :] * dext[off:off + RD, :]
            acc = term if acc is None else acc + term
        y = _silu(acc)
        for j in range(3 * DN_HEADS):
            seg = y[:, j * DN_HEAD_DIM:(j + 1) * DN_HEAD_DIM]
            if j < 2 * DN_HEADS:
                seg = seg * lax.rsqrt(jnp.sum(seg * seg, axis=-1, keepdims=True) + EPS)
            qkv_ref[r0:r0 + RD, j * DN_HEAD_DIM:(j + 1) * DN_HEAD_DIM] = seg

    @pl.when(t == NT - 1)
    def _():
        cbuf_ref[...] = cext[HC + R - c_halo:HC + R, :]
        dbuf_ref[...] = dext[HD + R - d_halo:HD + R, :]


def _premix(x, cbuf0, dbuf0, g, w_in_main, w_ba, cw, cb, lng, lnb, dw, alog, dtb, *, R, rs):
    G, L, _ = x.shape
    NT = L // R
    c_halo = (CONF_K - 1) * rs
    d_halo = (DN_CONV_K - 1) * rs
    HC = _round_up(c_halo, SUBLANES)
    HD = _round_up(d_halo, SUBLANES)

    def tile(c):
        return pl.BlockSpec((None, R, c), lambda gi, ti: (gi, ti, 0))

    def full(a):
        return pl.BlockSpec(a.shape, lambda gi, ti: (0,) * a.ndim)

    def per_group(rows, c):
        return pl.BlockSpec((None, rows, c), lambda gi, ti: (gi, 0, 0))

    weights = (g, w_in_main, w_ba, cw, cb, lng, lnb, dw, alog, dtb)
    return pl.pallas_call(
        functools.partial(_premix_kernel, R=R, rs=rs, NT=NT),
        grid=(G, NT),
        in_specs=[tile(D_MODEL)] + [full(a) for a in weights]
        + [per_group(c_halo, CONF_WIDTH), per_group(d_halo, 3 * DN_WIDTH)],
        out_specs=[tile(CONF_WIDTH), tile(3 * DN_WIDTH), tile(DN_WIDTH), tile(LANES),
                   per_group(c_halo, CONF_WIDTH), per_group(d_halo, 3 * DN_WIDTH)],
        out_shape=[jax.ShapeDtypeStruct((G, L, CONF_WIDTH), F32),
                   jax.ShapeDtypeStruct((G, L, 3 * DN_WIDTH), F32),
                   jax.ShapeDtypeStruct((G, L, DN_WIDTH), F32),
                   jax.ShapeDtypeStruct((G, L, LANES), F32),
                   jax.ShapeDtypeStruct((G, c_halo, CONF_WIDTH), F32),
                   jax.ShapeDtypeStruct((G, d_halo, 3 * DN_WIDTH), F32)],
        scratch_shapes=[pltpu.VMEM((HC + R, CONF_WIDTH), F32), pltpu.VMEM((HD + R, 3 * DN_WIDTH), F32)],
        compiler_params=pltpu.CompilerParams(dimension_semantics=("arbitrary", "arbitrary"),
                                             vmem_limit_bytes=VMEM_LIMIT),
        name="premix",
    )(x, *weights, cbuf0, dbuf0)


def _bmm(a, b):
    return jnp.einsum("nij,njk->nik", a.astype(BF16), b.astype(BF16), preferred_element_type=F32)


def _bmm_nt(a, b):
    return jnp.einsum("nid,njd->nij", a.astype(BF16), b.astype(BF16), preferred_element_type=F32)


def _bmm_tn(a, b):
    return jnp.einsum("nck,ncv->nkv", a.astype(BF16), b.astype(BF16), preferred_element_type=F32)


def _neumann(a, eye, squarings, mm=_mm):
    x = eye - a
    p = a
    for _ in range(squarings):
        p = mm(p, p)
        x = x + mm(x, p)
    return x


def _unit_lower_inverse_64(a, ri, ci):
    eye = (ri == ci).astype(F32)
    same_block = lax.shift_right_logical(ri, 4) == lax.shift_right_logical(ci, 4)
    a_diag = jnp.where(same_block, a, 0.0)
    a_off = jnp.where(same_block, 0.0, a)
    d = _neumann(a_diag, eye, 3, _bmm)
    b = _bmm(d, a_off)
    return _bmm(_neumann(b, eye, 1, _bmm), d)


def _head_norm_gate(o, z, ng):
    return o * lax.rsqrt(jnp.mean(o * o, axis=-1, keepdims=True) + EPS) * ng * _silu(z)


def _delta_prompt_kernel(qkv_ref, z_ref, bg_ref, ng_ref, s0_ref, o_ref, snew_ref, s_scr, *, R, NT):
    t = pl.program_id(1)

    @pl.when(t == 0)
    def _():
        s_scr[...] = s0_ref[...]

    bg = bg_ref[...]
    ri = lax.broadcasted_iota(jnp.int32, (R, R), 0)
    ci = lax.broadcasted_iota(jnp.int32, (R, R), 1)
    same_chunk = lax.shift_right_logical(ri, 6) == lax.shift_right_logical(ci, 6)
    csum = jnp.where(same_chunk & (ci <= ri), 1.0, 0.0).astype(BF16)
    gc_all = _mm_exact_lhs(csum, bg)
    gc_t = gc_all.T

    ri64 = lax.broadcasted_iota(jnp.int32, (CHUNK, CHUNK), 0)
    ci64 = lax.broadcasted_iota(jnp.int32, (CHUNK, CHUNK), 1)
    causal = ri64 >= ci64
    strict = ri64 > ci64
    scale = DN_HEAD_DIM ** -0.5
    NC = R // CHUNK
    blocks = [(c * CHUNK, h) for c in range(NC) for h in range(DN_HEADS)]

    def heads(col0):
        return jnp.stack([qkv_ref[r0:r0 + CHUNK, col0 + h * DN_HEAD_DIM:col0 + (h + 1) * DN_HEAD_DIM]
                          for r0, h in blocks])

    q = heads(0) * scale
    k = heads(DN_WIDTH)
    v = heads(2 * DN_WIDTH)
    beta = jnp.stack([bg[r0:r0 + CHUNK, h:h + 1] for r0, h in blocks])
    gcc = jnp.stack([gc_all[r0:r0 + CHUNK, DN_HEADS + h:DN_HEADS + h + 1] for r0, h in blocks])
    gcr = jnp.stack([gc_t[DN_HEADS + h:DN_HEADS + h + 1, r0:r0 + CHUNK] for r0, h in blocks])
    glast = gcc[:, CHUNK - 1:CHUNK, :]
    decay = jnp.exp(jnp.where(causal, gcc - gcr, -jnp.inf))
    kb = k * beta
    a = jnp.where(strict, _bmm_nt(kb, k) * decay, 0.0)
    tinv = _unit_lower_inverse_64(a, ri64, ci64)
    eg = jnp.exp(gcc)
    uw = _bmm(tinv, jnp.concatenate([v * beta, kb * eg], axis=2))
    u = uw[:, :, :DN_HEAD_DIM]
    w = uw[:, :, DN_HEAD_DIM:]
    qk = jnp.where(causal, _bmm_nt(q, k) * decay, 0.0)
    qd = q * eg
    kd = k * jnp.exp(glast - gcc)
    eglast = jnp.exp(glast)

    s = s_scr[...]
    for c in range(NC):
        sl = slice(c * DN_HEADS, (c + 1) * DN_HEADS)
        v_new = u[sl] - _bmm(w[sl], s)
        o = _bmm(qd[sl], s) + _bmm(qk[sl], v_new)
        s = s * eglast[sl] + _bmm_tn(kd[sl], v_new)
        r0 = c * CHUNK
        for h in range(DN_HEADS):
            lo = h * DN_HEAD_DIM
            o_ref[r0:r0 + CHUNK, lo:lo + DN_HEAD_DIM] = _head_norm_gate(
                o[h], z_ref[r0:r0 + CHUNK, lo:lo + DN_HEAD_DIM], ng_ref[...])
    s_scr[...] = s

    @pl.when(t == NT - 1)
    def _():
        snew_ref[...] = s_scr[...]


def _delta_prompt(qkv, z, bg, ng, s0, *, R):
    B, L, _ = qkv.shape
    NT = L // R

    def tile(c):
        return pl.BlockSpec((None, R, c), lambda bi, ti: (bi, ti, 0))

    s_spec = pl.BlockSpec((None, DN_HEADS, DN_HEAD_DIM, DN_HEAD_DIM), lambda bi, ti: (bi, 0, 0, 0))
    return pl.pallas_call(
        functools.partial(_delta_prompt_kernel, R=R, NT=NT),
        grid=(B, NT),
        in_specs=[tile(3 * DN_WIDTH), tile(DN_WIDTH), tile(LANES),
                  pl.BlockSpec(ng.shape, lambda bi, ti: (0, 0)), s_spec],
        out_specs=[tile(DN_WIDTH), s_spec],
        out_shape=[jax.ShapeDtypeStruct((B, L, DN_WIDTH), F32), jax.ShapeDtypeStruct(s0.shape, F32)],
        scratch_shapes=[pltpu.VMEM((DN_HEADS, DN_HEAD_DIM, DN_HEAD_DIM), F32)],
        compiler_params=pltpu.CompilerParams(dimension_semantics=("arbitrary", "arbitrary"),
                                             vmem_limit_bytes=VMEM_LIMIT),
        name="delta_prompt",
    )(qkv, z, bg, ng, s0)


def _delta_sample_kernel(qkv_ref, z_ref, bg_ref, ng_ref, s0_ref, o_ref, snew_ref,
                         w_scr, qd_scr, kd_scr, u_scr, vn_scr, os_scr, *, T, rs):
    R = T * rs
    shift = rs.bit_length() - 1
    bg = bg_ref[...]
    ri = lax.broadcasted_iota(jnp.int32, (R, R), 0)
    ci = lax.broadcasted_iota(jnp.int32, (R, R), 1)
    same_seq = (ri & (rs - 1)) == (ci & (rs - 1))
    ti = lax.shift_right_logical(ri, shift)
    tj = lax.shift_right_logical(ci, shift)
    causal = same_seq & (ti >= tj)
    strict = same_seq & (ti > tj)
    eye = (ri == ci).astype(F32)
    gc_all = _mm_exact_lhs(jnp.where(causal, 1.0, 0.0).astype(BF16), bg)
    gc_t = gc_all.T
    scale = DN_HEAD_DIM ** -0.5
    squarings = (T - 1).bit_length() - 1

    for h in range(DN_HEADS):
        lo = h * DN_HEAD_DIM
        q = qkv_ref[:, lo:lo + DN_HEAD_DIM] * scale
        k = qkv_ref[:, DN_WIDTH + lo:DN_WIDTH + lo + DN_HEAD_DIM]
        v = qkv_ref[:, 2 * DN_WIDTH + lo:2 * DN_WIDTH + lo + DN_HEAD_DIM]
        beta = bg[:, h:h + 1]
        gcc = gc_all[:, DN_HEADS + h:DN_HEADS + h + 1]
        gcr = gc_t[DN_HEADS + h:DN_HEADS + h + 1, :]
        glast = gc_all[R - rs:R, DN_HEADS + h:DN_HEADS + h + 1]
        glast_rows = jnp.concatenate([glast] * T, axis=0)
        decay = jnp.exp(jnp.where(causal, gcc - gcr, -jnp.inf))
        kb = k * beta
        a = jnp.where(strict, _mm_nt(kb, k) * decay, 0.0)
        tinv = _neumann(a, eye, squarings)
        eg = jnp.exp(gcc)
        uw = _mm(tinv, jnp.concatenate([v * beta, kb * eg], axis=1))
        u_scr[...] = uw[:, :DN_HEAD_DIM]
        w_scr[...] = uw[:, DN_HEAD_DIM:]
        qd_scr[...] = q * eg
        kd_scr[...] = k * jnp.exp(glast_rows - gcc)
        qk = jnp.where(causal, _mm_nt(q, k) * decay, 0.0)
        eglast = jnp.exp(glast)
        for s in range(rs):
            rows = pl.ds(s, T, stride=rs)
            state = s0_ref[s, h]
            ws = _mm(jnp.concatenate([w_scr[rows, :], qd_scr[rows, :]], axis=0), state)
            v_new = u_scr[rows, :] - ws[:T]
            vn_scr[rows, :] = v_new
            os_scr[rows, :] = ws[T:]
            snew_ref[s, h] = state * eglast[s:s + 1, :] + _mm_tn(kd_scr[rows, :], v_new)
        o = os_scr[...] + _mm(qk, vn_scr[...])
        o_ref[:, lo:lo + DN_HEAD_DIM] = _head_norm_gate(o, z_ref[:, lo:lo + DN_HEAD_DIM], ng_ref[...])


def _delta_sample(qkv, z, bg, ng, s0, *, T, rs):
    G, R, _ = qkv.shape

    def tile(c):
        return pl.BlockSpec((None, R, c), lambda gi: (gi, 0, 0))

    s_spec = pl.BlockSpec((rs, DN_HEADS, DN_HEAD_DIM, DN_HEAD_DIM), lambda gi: (gi, 0, 0, 0))
    return pl.pallas_call(
        functools.partial(_delta_sample_kernel, T=T, rs=rs),
        grid=(G,),
        in_specs=[tile(3 * DN_WIDTH), tile(DN_WIDTH), tile(LANES),
                  pl.BlockSpec(ng.shape, lambda gi: (0, 0)), s_spec],
        out_specs=[tile(DN_WIDTH), s_spec],
        out_shape=[jax.ShapeDtypeStruct((G, R, DN_WIDTH), F32), jax.ShapeDtypeStruct(s0.shape, F32)],
        scratch_shapes=[pltpu.VMEM((R, DN_HEAD_DIM), F32) for _ in range(6)],
        compiler_params=pltpu.CompilerParams(dimension_semantics=("arbitrary",),
                                             vmem_limit_bytes=VMEM_LIMIT),
        name="delta_sample",
    )(qkv, z, bg, ng, s0)


def _postmix_kernel(x_ref, conf_ref, o_ref, p_ref, fbuf0_ref, wout_ref, gffn_ref, wup_ref, fw_ref, fb_ref,
                    wdown_ref, wple_ref, wpg_ref, gfin_ref,
                    y_ref, fbuf_ref, fext, act_scr, *, R, rs, NT):
    t = pl.program_id(1)
    halo = (FFN_CONV_K - 1) * rs
    HF = _round_up(halo, SUBLANES)

    @pl.when(t == 0)
    def _():
        fext[HF - halo:HF, :] = fbuf0_ref[...]

    if NT > 1:
        @pl.when(t > 0)
        def _():
            fext[0:HF, :] = fext[R:R + HF, :]

    mix = jnp.concatenate([conf_ref[...].astype(BF16), o_ref[...].astype(BF16)], axis=1)
    x1 = x_ref[...] + jnp.dot(mix, wout_ref[...], preferred_element_type=F32)
    hn = _rms(x1, gffn_ref[...]).astype(BF16)
    for j in range(2 * N_FF_CHUNKS):
        c0 = j * FF_COLS
        fext[HF:HF + R, c0:c0 + FF_COLS] = jnp.dot(hn, wup_ref[:, c0:c0 + FF_COLS],
                                                   preferred_element_type=F32)

    RC = 64
    for j in range(N_FF_CHUNKS):
        for r0 in range(0, R, RC):
            halves = []
            for c0 in (j * FF_COLS, D_FF + j * FF_COLS):
                acc = jnp.broadcast_to(fb_ref[:, c0:c0 + FF_COLS], (RC, FF_COLS))
                for k in range(FFN_CONV_K):
                    off = HF - (FFN_CONV_K - 1 - k) * rs + r0
                    acc = acc + fw_ref[k:k + 1, c0:c0 + FF_COLS] * fext[off:off + RC, c0:c0 + FF_COLS]
                halves.append(acc)
            act_scr[r0:r0 + RC, j * FF_COLS:(j + 1) * FF_COLS] = (_silu(halves[0]) * halves[1]).astype(BF16)

    x2 = x1 + jnp.dot(act_scr[...], wdown_ref[...], preferred_element_type=F32)
    ple = jnp.dot(p_ref[...].astype(BF16), wple_ref[...], preferred_element_type=F32)
    x3 = x2 + ple * _sigmoid(jnp.dot(x2.astype(BF16), wpg_ref[...], preferred_element_type=F32))
    y_ref[...] = _rms(x3, gfin_ref[...])

    @pl.when(t == NT - 1)
    def _():
        fbuf_ref[...] = fext[HF + R - halo:HF + R, :]


def _postmix(x, conf, o, p, fbuf0, w_out, g_ffn, w_up, fw, fb, w_down, w_ple, w_pg, g_fin, *, R, rs):
    G, L, _ = x.shape
    NT = L // R
    halo = (FFN_CONV_K - 1) * rs
    HF = _round_up(halo, SUBLANES)

    def tile(c):
        return pl.BlockSpec((None, R, c), lambda gi, ti: (gi, ti, 0))

    def full(a):
        return pl.BlockSpec(a.shape, lambda gi, ti: (0,) * a.ndim)

    buf_spec = pl.BlockSpec((None, halo, 2 * D_FF), lambda gi, ti: (gi, 0, 0))
    weights = (w_out, g_ffn, w_up, fw, fb, w_down, w_ple, w_pg, g_fin)
    return pl.pallas_call(
        functools.partial(_postmix_kernel, R=R, rs=rs, NT=NT),
        grid=(G, NT),
        in_specs=[tile(D_MODEL), tile(CONF_WIDTH), tile(DN_WIDTH), tile(D_PLE), buf_spec]
        + [full(a) for a in weights],
        out_specs=[tile(D_MODEL), buf_spec],
        out_shape=[jax.ShapeDtypeStruct((G, L, D_MODEL), F32),
                   jax.ShapeDtypeStruct((G, halo, 2 * D_FF), F32)],
        scratch_shapes=[pltpu.VMEM((HF + R, 2 * D_FF), F32), pltpu.VMEM((R, D_FF), BF16)],
        compiler_params=pltpu.CompilerParams(dimension_semantics=("arbitrary", "arbitrary"),
                                             vmem_limit_bytes=VMEM_LIMIT),
        name="postmix",
    )(x, conf, o, p, fbuf0, *weights)


def _interleave(a, rs):
    n, t, c = a.shape
    return a.reshape(n // rs, rs, t, c).transpose(0, 2, 1, 3).reshape(n // rs, t * rs, c)


def _deinterleave(a, rs, t):
    g, _, c = a.shape
    return a.reshape(g, t, rs, c).transpose(0, 2, 1, 3).reshape(g * rs, t, c)


def _trunk(x, p, conf_buf, dn_buf, s0, ffn_buf, w, *, interleaved):
    n, seq, _ = x.shape
    if interleaved:
        rs = SAMPLE_RS
        x, p, conf_buf, dn_buf, ffn_buf = (_interleave(a, rs) for a in (x, p, conf_buf, dn_buf, ffn_buf))
        R = seq * rs
    else:
        rs = 1
        R = 512
    conf, qkv, z, bg, conf_new, dn_new = _premix(
        x, conf_buf, dn_buf, w["g_mix"], w["w_in_main"], w["w_ba"], w["conf_w"], w["conf_b"],
        w["ln_g"], w["ln_b"], w["dn_w"], w["a_log"], w["dt_bias"], R=R, rs=rs)
    if interleaved:
        o, s_new = _delta_sample(qkv, z, bg, w["dn_norm_g"], s0, T=seq, rs=rs)
    else:
        o, s_new = _delta_prompt(qkv, z, bg, w["dn_norm_g"], s0, R=256)
    y, ffn_new = _postmix(x, conf, o, p, ffn_buf, w["w_out"], w["g_ffn"], w["w_up"], w["ffn_w"], w["ffn_b"],
                          w["w_down"], w["w_ple"], w["w_pg"], w["g_fin"], R=min(R, 256), rs=rs)
    if interleaved:
        y = _deinterleave(y, rs, seq)
        conf_new = _deinterleave(conf_new, rs, CONF_K - 1)
        dn_new = _deinterleave(dn_new, rs, DN_CONV_K - 1)
        ffn_new = _deinterleave(ffn_new, rs, FFN_CONV_K - 1)
    return y, conf_new[None], dn_new[None], s_new[None], ffn_new[None]


def kernel(x_prompt, x_sample, p_prompt, p_sample, state_conf_buf, state_dn_conv_buf, state_dn_S, state_ffn_buf,
           norm_mix_g, w_in, conf_dw_w, conf_dw_b, conf_ln_g, conf_ln_b, dn_conv_w, dn_a_log, dn_dt_bias,
           dn_norm_g, w_out, norm_ffn_g, w_up, ffn_conv_w, ffn_conv_b, w_down, w_ple, w_ple_gate, norm_final_g):
    assert w_in.shape[0] == 1, "single trunk layer"
    n_main = 2 * CONF_WIDTH + 4 * DN_WIDTH
    gate_pad = ((0, 0), (DN_HEADS, LANES - 2 * DN_HEADS))
    w = dict(
        g_mix=norm_mix_g[0][None], w_in_main=w_in[0][:, :n_main].astype(BF16),
        w_ba=jnp.pad(w_in[0][:, n_main:], ((0, 0), (0, LANES - 2 * DN_HEADS))).astype(BF16),
        conf_w=conf_dw_w[0], conf_b=conf_dw_b[0][None], ln_g=conf_ln_g[0][None], ln_b=conf_ln_b[0][None],
        dn_w=dn_conv_w[0], a_log=jnp.pad(dn_a_log[0][None], gate_pad), dt_bias=jnp.pad(dn_dt_bias[0][None], gate_pad),
        dn_norm_g=dn_norm_g[0][None], w_out=w_out[0].astype(BF16), g_ffn=norm_ffn_g[0][None],
        w_up=w_up[0].astype(BF16), ffn_w=ffn_conv_w[0], ffn_b=ffn_conv_b[0][None], w_down=w_down[0].astype(BF16),
        w_ple=w_ple[0].astype(BF16), w_pg=w_ple_gate[0].astype(BF16), g_fin=norm_final_g[None],
    )
    bp, dt = x_prompt.shape[0], x_prompt.dtype
    z_conf = jnp.zeros((bp, CONF_K - 1, CONF_WIDTH), dt)
    z_dn = jnp.zeros((bp, DN_CONV_K - 1, 3 * DN_WIDTH), dt)
    z_s = jnp.zeros((bp, DN_HEADS, DN_HEAD_DIM, DN_HEAD_DIM), dt)
    z_ffn = jnp.zeros((bp, FFN_CONV_K - 1, 2 * D_FF), dt)
    yp, pc, pd, ps, pf = _trunk(x_prompt, p_prompt[0], z_conf, z_dn, z_s, z_ffn, w, interleaved=False)
    ys, sc, sd, ss, sf = _trunk(x_sample, p_sample[0], state_conf_buf[0], state_dn_conv_buf[0], state_dn_S[0],
                                state_ffn_buf[0], w, interleaved=True)
    return (yp, ys, pc, pd, ps, pf, sc, sd, ss, sf)
```

```python
import functools

import jax
import jax.numpy as jnp
from jax import lax
from jax.experimental import pallas as pl
from jax.experimental.pallas import tpu as pltpu

F32 = jnp.float32
BF16 = jnp.bfloat16

D_MODEL = 1024
D_PLE = 256
CONF_WIDTH = 512
CONF_K = 31
DN_HEADS = 4
DN_HEAD_DIM = 128
DN_WIDTH = DN_HEADS * DN_HEAD_DIM
DN_CONV_K = 4
D_FF = 2816
FFN_CONV_K = 3
CHUNK = 64
EPS = 1e-6

SUBLANES = 8
LANES = 128
FF_COLS = 256
N_FF_CHUNKS = D_FF // FF_COLS
SAMPLE_RS = 32
VMEM_LIMIT = 56 * 1024 * 1024


def _round_up(n, m):
    return (n + m - 1) // m * m


def _mm(a, b):
    return jnp.dot(a.astype(BF16), b.astype(BF16), preferred_element_type=F32)


def _mm_nt(a, b):
    return lax.dot_general(a.astype(BF16), b.astype(BF16), (((1,), (1,)), ((), ())),
                           preferred_element_type=F32)


def _mm_tn(a, b):
    return lax.dot_general(a.astype(BF16), b.astype(BF16), (((0,), (0,)), ((), ())),
                           preferred_element_type=F32)


def _split3(x):
    hi = x.astype(BF16)
    r1 = x - hi.astype(F32)
    mid = r1.astype(BF16)
    lo = (r1 - mid.astype(F32)).astype(BF16)
    return hi, mid, lo


def _mm_exact_lhs(m_bf16, x):
    hi, mid, lo = _split3(x)
    dot = functools.partial(jnp.dot, preferred_element_type=F32)
    return dot(m_bf16, hi) + dot(m_bf16, mid) + dot(m_bf16, lo)


def _sigmoid(x):
    return 1.0 / (1.0 + jnp.exp(-x))


def _silu(x):
    return x * _sigmoid(x)


def _softplus(x):
    return jnp.maximum(x, 0.0) + jnp.log1p(jnp.exp(-jnp.abs(x)))


def _rms(x, g):
    return x * lax.rsqrt(jnp.mean(x * x, axis=-1, keepdims=True) + EPS) * g


def _causal_conv(ext_ref, w_ref, cols, taps, rs, halo_rows, r0, n_rows):
    groups = {}
    for k in range(taps):
        off = halo_rows - (taps - 1 - k) * rs
        groups.setdefault(off % SUBLANES, []).append((k, off - off % SUBLANES))
    total = None
    for mis, members in sorted(groups.items()):
        n = n_rows + (SUBLANES if mis else 0)
        part = None
        for k, base in members:
            term = pltpu.repeat(w_ref[k, :, cols], n // SUBLANES, axis=0) * ext_ref[base + r0:base + r0 + n, cols]
            part = term if part is None else part + term
        if mis:
            part = part[mis:mis + n_rows]
        total = part if total is None else total + part
    return total


def _premix_kernel(x_ref, g_ref, win_ref, wba_ref, cw_ref, cb_ref, lng_ref, lnb_ref, dw_ref,
                   alog_ref, dtb_ref, cbuf0_ref, dbuf0_ref,
                   conf_ref, qkv_ref, z_ref, bg_ref, cbuf_ref, dbuf_ref,
                   cext, dext, *, R, rs, NT):
    t = pl.program_id(1)
    c_halo = (CONF_K - 1) * rs
    d_halo = (DN_CONV_K - 1) * rs
    HC = _round_up(c_halo, SUBLANES)
    HD = _round_up(d_halo, SUBLANES)

    @pl.when(t == 0)
    def _():
        cext[HC - c_halo:HC, :] = cbuf0_ref[...]
        dext[HD - d_halo:HD, :] = dbuf0_ref[...]

    if NT > 1:
        @pl.when(t > 0)
        def _():
            cext[0:HC, :] = cext[R:R + HC, :]
            dext[0:HD, :] = dext[R:R + HD, :]

    h = _rms(x_ref[...], g_ref[...]).astype(BF16)
    vg = jnp.dot(h, win_ref[:, 0:2 * CONF_WIDTH], preferred_element_type=F32)
    cext[HC:HC + R, :] = vg[:, :CONF_WIDTH] * _sigmoid(vg[:, CONF_WIDTH:])
    dext[HD:HD + R, :] = jnp.dot(h, win_ref[:, 2 * CONF_WIDTH:2 * CONF_WIDTH + 3 * DN_WIDTH],
                                 preferred_element_type=F32)
    z_ref[...] = jnp.dot(h, win_ref[:, 2 * CONF_WIDTH + 3 * DN_WIDTH:], preferred_element_type=F32)

    ba = jnp.dot(h, wba_ref[...], preferred_element_type=F32)
    lane = lax.broadcasted_iota(jnp.int32, ba.shape, 1)
    gate = -jnp.exp(alog_ref[...]) * _softplus(ba + dtb_ref[...])
    bg_ref[...] = jnp.where(lane < DN_HEADS, _sigmoid(ba), gate)

    RC = 64
    for r0 in range(0, R, RC):
        acc = _causal_conv(cext, cw_ref, slice(None), CONF_K, rs, HC, r0, RC) + cb_ref[...]
        mu = jnp.mean(acc, axis=-1, keepdims=True)
        cen = acc - mu
        var = jnp.mean(cen * cen, axis=-1, keepdims=True)
        n = cen * lax.rsqrt(var + EPS) * lng_ref[...] + lnb_ref[...]
        conf_ref[r0:r0 + RC, :] = _silu(n)

    RD = 32
    for r0 in range(0, R, RD):
        y = _silu(_causal_conv(dext, dw_ref, slice(None), DN_CONV_K, rs, HD, r0, RD))
        for j in range(3 * DN_HEADS):
            seg = y[:, j * DN_HEAD_DIM:(j + 1) * DN_HEAD_DIM]
            if j < 2 * DN_HEADS:
                seg = seg * lax.rsqrt(jnp.sum(seg * seg, axis=-1, keepdims=True) + EPS)
            qkv_ref[r0:r0 + RD, j * DN_HEAD_DIM:(j + 1) * DN_HEAD_DIM] = seg

    @pl.when(t == NT - 1)
    def _():
        cbuf_ref[...] = cext[HC + R - c_halo:HC + R, :]
        dbuf_ref[...] = dext[HD + R - d_halo:HD + R, :]


def _premix(x, cbuf0, dbuf0, g, w_in_main, w_ba, cw, cb, lng, lnb, dw, alog, dtb, *, R, rs):
    G, L, _ = x.shape
    NT = L // R
    c_halo = (CONF_K - 1) * rs
    d_halo = (DN_CONV_K - 1) * rs
    HC = _round_up(c_halo, SUBLANES)
    HD = _round_up(d_halo, SUBLANES)

    def tile(c):
        return pl.BlockSpec((None, R, c), lambda gi, ti: (gi, ti, 0))

    def full(a):
        return pl.BlockSpec(a.shape, lambda gi, ti: (0,) * a.ndim)

    def per_group(rows, c):
        return pl.BlockSpec((None, rows, c), lambda gi, ti: (gi, 0, 0))

    weights = (g, w_in_main, w_ba, cw, cb, lng, lnb, dw, alog, dtb)
    return pl.pallas_call(
        functools.partial(_premix_kernel, R=R, rs=rs, NT=NT),
        grid=(G, NT),
        in_specs=[tile(D_MODEL)] + [full(a) for a in weights]
        + [per_group(c_halo, CONF_WIDTH), per_group(d_halo, 3 * DN_WIDTH)],
        out_specs=[tile(CONF_WIDTH), tile(3 * DN_WIDTH), tile(DN_WIDTH), tile(LANES),
                   per_group(c_halo, CONF_WIDTH), per_group(d_halo, 3 * DN_WIDTH)],
        out_shape=[jax.ShapeDtypeStruct((G, L, CONF_WIDTH), F32),
                   jax.ShapeDtypeStruct((G, L, 3 * DN_WIDTH), F32),
                   jax.ShapeDtypeStruct((G, L, DN_WIDTH), F32),
                   jax.ShapeDtypeStruct((G, L, LANES), F32),
                   jax.ShapeDtypeStruct((G, c_halo, CONF_WIDTH), F32),
                   jax.ShapeDtypeStruct((G, d_halo, 3 * DN_WIDTH), F32)],
        scratch_shapes=[pltpu.VMEM((HC + R, CONF_WIDTH), F32), pltpu.VMEM((HD + R, 3 * DN_WIDTH), F32)],
        compiler_params=pltpu.CompilerParams(dimension_semantics=("arbitrary", "arbitrary"),
                                             vmem_limit_bytes=VMEM_LIMIT),
        name="premix",
    )(x, *weights, cbuf0, dbuf0)


def _bmm(a, b):
    return jnp.einsum("nij,njk->nik", a.astype(BF16), b.astype(BF16), preferred_element_type=F32)


def _bmm_nt(a, b):
    return jnp.einsum("nid,njd->nij", a.astype(BF16), b.astype(BF16), preferred_element_type=F32)


def _bmm_tn(a, b):
    return jnp.einsum("nck,ncv->nkv", a.astype(BF16), b.astype(BF16), preferred_element_type=F32)


def _neumann(a, eye, squarings, mm=_mm):
    x = eye - a
    p = a
    for _ in range(squarings):
        p = mm(p, p)
        x = x + mm(x, p)
    return x


def _unit_lower_inverse_64(a, ri, ci):
    eye = (ri == ci).astype(F32)
    same_block = lax.shift_right_logical(ri, 4) == lax.shift_right_logical(ci, 4)
    a_diag = jnp.where(same_block, a, 0.0)
    a_off = jnp.where(same_block, 0.0, a)
    d = _neumann(a_diag, eye, 3, _bmm)
    b = _bmm(d, a_off)
    return _bmm(_neumann(b, eye, 1, _bmm), d)


def _head_norm_gate(o, z, ng):
    return o * lax.rsqrt(jnp.mean(o * o, axis=-1, keepdims=True) + EPS) * ng * _silu(z)


def _delta_prompt_kernel(qkv_ref, z_ref, bg_ref, ng_ref, s0_ref, o_ref, snew_ref, s_scr, *, R, NT):
    t = pl.program_id(1)

    @pl.when(t == 0)
    def _():
        s_scr[...] = s0_ref[...]

    bg = bg_ref[...]
    ri = lax.broadcasted_iota(jnp.int32, (R, R), 0)
    ci = lax.broadcasted_iota(jnp.int32, (R, R), 1)
    same_chunk = lax.shift_right_logical(ri, 6) == lax.shift_right_logical(ci, 6)
    csum = jnp.where(same_chunk & (ci <= ri), 1.0, 0.0).astype(BF16)
    gc_all = _mm_exact_lhs(csum, bg)
    gc_t = gc_all.T

    ri64 = lax.broadcasted_iota(jnp.int32, (CHUNK, CHUNK), 0)
    ci64 = lax.broadcasted_iota(jnp.int32, (CHUNK, CHUNK), 1)
    causal = ri64 >= ci64
    strict = ri64 > ci64
    scale = DN_HEAD_DIM ** -0.5
    NC = R // CHUNK
    blocks = [(c * CHUNK, h) for c in range(NC) for h in range(DN_HEADS)]

    def heads(col0):
        return jnp.stack([qkv_ref[r0:r0 + CHUNK, col0 + h * DN_HEAD_DIM:col0 + (h + 1) * DN_HEAD_DIM]
                          for r0, h in blocks])

    q = heads(0) * scale
    k = heads(DN_WIDTH)
    v = heads(2 * DN_WIDTH)
    beta = jnp.stack([bg[r0:r0 + CHUNK, h:h + 1] for r0, h in blocks])
    gcc = jnp.stack([gc_all[r0:r0 + CHUNK, DN_HEADS + h:DN_HEADS + h + 1] for r0, h in blocks])
    gcr = jnp.stack([gc_t[DN_HEADS + h:DN_HEADS + h + 1, r0:r0 + CHUNK] for r0, h in blocks])
    glast = gcc[:, CHUNK - 1:CHUNK, :]
    decay = jnp.exp(jnp.where(causal, gcc - gcr, -jnp.inf))
    kb = k * beta
    a = jnp.where(strict, _bmm_nt(kb, k) * decay, 0.0)
    tinv = _unit_lower_inverse_64(a, ri64, ci64)
    eg = jnp.exp(gcc)
    uw = _bmm(tinv, jnp.concatenate([v * beta, kb * eg], axis=2))
    u = uw[:, :, :DN_HEAD_DIM]
    w = uw[:, :, DN_HEAD_DIM:]
    qk = jnp.where(causal, _bmm_nt(q, k) * decay, 0.0)
    qd = q * eg
    kd = k * jnp.exp(glast - gcc)
    eglast = jnp.exp(glast)

    s = s_scr[...]
    for c in range(NC):
        sl = slice(c * DN_HEADS, (c + 1) * DN_HEADS)
        v_new = u[sl] - _bmm(w[sl], s)
        o = _bmm(qd[sl], s) + _bmm(qk[sl], v_new)
        s = s * eglast[sl] + _bmm_tn(kd[sl], v_new)
        r0 = c * CHUNK
        for h in range(DN_HEADS):
            lo = h * DN_HEAD_DIM
            o_ref[r0:r0 + CHUNK, lo:lo + DN_HEAD_DIM] = _head_norm_gate(
                o[h], z_ref[r0:r0 + CHUNK, lo:lo + DN_HEAD_DIM], ng_ref[...])
    s_scr[...] = s

    @pl.when(t == NT - 1)
    def _():
        snew_ref[...] = s_scr[...]


def _delta_prompt(qkv, z, bg, ng, s0, *, R):
    B, L, _ = qkv.shape
    NT = L // R

    def tile(c):
        return pl.BlockSpec((None, R, c), lambda bi, ti: (bi, ti, 0))

    s_spec = pl.BlockSpec((None, DN_HEADS, DN_HEAD_DIM, DN_HEAD_DIM), lambda bi, ti: (bi, 0, 0, 0))
    return pl.pallas_call(
        functools.partial(_delta_prompt_kernel, R=R, NT=NT),
        grid=(B, NT),
        in_specs=[tile(3 * DN_WIDTH), tile(DN_WIDTH), tile(LANES),
                  pl.BlockSpec(ng.shape, lambda bi, ti: (0, 0)), s_spec],
        out_specs=[tile(DN_WIDTH), s_spec],
        out_shape=[jax.ShapeDtypeStruct((B, L, DN_WIDTH), F32), jax.ShapeDtypeStruct(s0.shape, F32)],
        scratch_shapes=[pltpu.VMEM((DN_HEADS, DN_HEAD_DIM, DN_HEAD_DIM), F32)],
        compiler_params=pltpu.CompilerParams(dimension_semantics=("arbitrary", "arbitrary"),
                                             vmem_limit_bytes=VMEM_LIMIT),
        name="delta_prompt",
    )(qkv, z, bg, ng, s0)


def _delta_sample_kernel(qkv_ref, z_ref, bg_ref, ng_ref, s0_ref, o_ref, snew_ref,
                         w_scr, qd_scr, kd_scr, u_scr, vn_scr, os_scr, *, T, rs):
    R = T * rs
    shift = rs.bit_length() - 1
    bg = bg_ref[...]
    ri = lax.broadcasted_iota(jnp.int32, (R, R), 0)
    ci = lax.broadcasted_iota(jnp.int32, (R, R), 1)
    same_seq = (ri & (rs - 1)) == (ci & (rs - 1))
    ti = lax.shift_right_logical(ri, shift)
    tj = lax.shift_right_logical(ci, shift)
    causal = same_seq & (ti >= tj)
    strict = same_seq & (ti > tj)
    eye = (ri == ci).astype(F32)
    gc_all = _mm_exact_lhs(jnp.where(causal, 1.0, 0.0).astype(BF16), bg)
    gc_t = gc_all.T
    scale = DN_HEAD_DIM ** -0.5
    squarings = (T - 1).bit_length() - 1

    for h in range(DN_HEADS):
        lo = h * DN_HEAD_DIM
        q = qkv_ref[:, lo:lo + DN_HEAD_DIM] * scale
        k = qkv_ref[:, DN_WIDTH + lo:DN_WIDTH + lo + DN_HEAD_DIM]
        v = qkv_ref[:, 2 * DN_WIDTH + lo:2 * DN_WIDTH + lo + DN_HEAD_DIM]
        beta = bg[:, h:h + 1]
        gcc = gc_all[:, DN_HEADS + h:DN_HEADS + h + 1]
        gcr = gc_t[DN_HEADS + h:DN_HEADS + h + 1, :]
        glast = gc_all[R - rs:R, DN_HEADS + h:DN_HEADS + h + 1]
        glast_rows = jnp.concatenate([glast] * T, axis=0)
        decay = jnp.exp(jnp.where(causal, gcc - gcr, -jnp.inf))
        kb = k * beta
        a = jnp.where(strict, _mm_nt(kb, k) * decay, 0.0)
        tinv = _neumann(a, eye, squarings)
        eg = jnp.exp(gcc)
        uw = _mm(tinv, jnp.concatenate([v * beta, kb * eg], axis=1))
        u_scr[...] = uw[:, :DN_HEAD_DIM]
        w_scr[...] = uw[:, DN_HEAD_DIM:]
        qd_scr[...] = q * eg
        kd_scr[...] = k * jnp.exp(glast_rows - gcc)
        qk = jnp.where(causal, _mm_nt(q, k) * decay, 0.0)
        eglast = jnp.exp(glast)
        for s in range(rs):
            rows = pl.ds(s, T, stride=rs)
            state = s0_ref[s, h]
            ws = _mm(jnp.concatenate([w_scr[rows, :], qd_scr[rows, :]], axis=0), state)
            v_new = u_scr[rows, :] - ws[:T]
            vn_scr[rows, :] = v_new
            os_scr[rows, :] = ws[T:]
            snew_ref[s, h] = state * eglast[s:s + 1, :] + _mm_tn(kd_scr[rows, :], v_new)
        o = os_scr[...] + _mm(qk, vn_scr[...])
        o_ref[:, lo:lo + DN_HEAD_DIM] = _head_norm_gate(o, z_ref[:, lo:lo + DN_HEAD_DIM], ng_ref[...])


def _delta_sample(qkv, z, bg, ng, s0, *, T, rs):
    G, R, _ = qkv.shape

    def tile(c):
        return pl.BlockSpec((None, R, c), lambda gi: (gi, 0, 0))

    s_spec = pl.BlockSpec((rs, DN_HEADS, DN_HEAD_DIM, DN_HEAD_DIM), lambda gi: (gi, 0, 0, 0))
    return pl.pallas_call(
        functools.partial(_delta_sample_kernel, T=T, rs=rs),
        grid=(G,),
        in_specs=[tile(3 * DN_WIDTH), tile(DN_WIDTH), tile(LANES),
                  pl.BlockSpec(ng.shape, lambda gi: (0, 0)), s_spec],
        out_specs=[tile(DN_WIDTH), s_spec],
        out_shape=[jax.ShapeDtypeStruct((G, R, DN_WIDTH), F32), jax.ShapeDtypeStruct(s0.shape, F32)],
        scratch_shapes=[pltpu.VMEM((R, DN_HEAD_DIM), F32) for _ in range(6)],
        compiler_params=pltpu.CompilerParams(dimension_semantics=("arbitrary",),
                                             vmem_limit_bytes=VMEM_LIMIT),
        name="delta_sample",
    )(qkv, z, bg, ng, s0)


def _postmix_kernel(x_ref, conf_ref, o_ref, p_ref, fbuf0_ref, wout_ref, gffn_ref, wup_ref, fw_ref, fb_ref,
                    wdown_ref, wple_ref, wpg_ref, gfin_ref,
                    y_ref, fbuf_ref, fext, act_scr, *, R, rs, NT):
    t = pl.program_id(1)
    halo = (FFN_CONV_K - 1) * rs
    HF = _round_up(halo, SUBLANES)

    @pl.when(t == 0)
    def _():
        fext[HF - halo:HF, :] = fbuf0_ref[...]

    if NT > 1:
        @pl.when(t > 0)
        def _():
            fext[0:HF, :] = fext[R:R + HF, :]

    mix = jnp.concatenate([conf_ref[...].astype(BF16), o_ref[...].astype(BF16)], axis=1)
    x1 = x_ref[...] + jnp.dot(mix, wout_ref[...], preferred_element_type=F32)
    hn = _rms(x1, gffn_ref[...]).astype(BF16)
    for j in range(2 * N_FF_CHUNKS):
        c0 = j * FF_COLS
        fext[HF:HF + R, c0:c0 + FF_COLS] = jnp.dot(hn, wup_ref[:, c0:c0 + FF_COLS],
                                                   preferred_element_type=F32)

    RC = 64
    for j in range(N_FF_CHUNKS):
        for r0 in range(0, R, RC):
            halves = []
            for c0 in (j * FF_COLS, D_FF + j * FF_COLS):
                cols = slice(c0, c0 + FF_COLS)
                halves.append(_causal_conv(fext, fw_ref, cols, FFN_CONV_K, rs, HF, r0, RC) + fb_ref[:, cols])
            act_scr[r0:r0 + RC, j * FF_COLS:(j + 1) * FF_COLS] = (_silu(halves[0]) * halves[1]).astype(BF16)

    x2 = x1 + jnp.dot(act_scr[...], wdown_ref[...], preferred_element_type=F32)
    ple = jnp.dot(p_ref[...].astype(BF16), wple_ref[...], preferred_element_type=F32)
    x3 = x2 + ple * _sigmoid(jnp.dot(x2.astype(BF16), wpg_ref[...], preferred_element_type=F32))
    y_ref[...] = _rms(x3, gfin_ref[...])

    @pl.when(t == NT - 1)
    def _():
        fbuf_ref[...] = fext[HF + R - halo:HF + R, :]


def _postmix(x, conf, o, p, fbuf0, w_out, g_ffn, w_up, fw, fb, w_down, w_ple, w_pg, g_fin, *, R, rs):
    G, L, _ = x.shape
    NT = L // R
    halo = (FFN_CONV_K - 1) * rs
    HF = _round_up(halo, SUBLANES)

    def tile(c):
        return pl.BlockSpec((None, R, c), lambda gi, ti: (gi, ti, 0))

    def full(a):
        return pl.BlockSpec(a.shape, lambda gi, ti: (0,) * a.ndim)

    buf_spec = pl.BlockSpec((None, halo, 2 * D_FF), lambda gi, ti: (gi, 0, 0))
    weights = (w_out, g_ffn, w_up, fw, fb, w_down, w_ple, w_pg, g_fin)
    return pl.pallas_call(
        functools.partial(_postmix_kernel, R=R, rs=rs, NT=NT),
        grid=(G, NT),
        in_specs=[tile(D_MODEL), tile(CONF_WIDTH), tile(DN_WIDTH), tile(D_PLE), buf_spec]
        + [full(a) for a in weights],
        out_specs=[tile(D_MODEL), buf_spec],
        out_shape=[jax.ShapeDtypeStruct((G, L, D_MODEL), F32),
                   jax.ShapeDtypeStruct((G, halo, 2 * D_FF), F32)],
        scratch_shapes=[pltpu.VMEM((HF + R, 2 * D_FF), F32), pltpu.VMEM((R, D_FF), BF16)],
        compiler_params=pltpu.CompilerParams(dimension_semantics=("arbitrary", "arbitrary"),
                                             vmem_limit_bytes=VMEM_LIMIT),
        name="postmix",
    )(x, conf, o, p, fbuf0, *weights)


def _sublane_replicated(w):
    return jnp.broadcast_to(w[:, None, :], (w.shape[0], SUBLANES, w.shape[1]))


def _interleave(a, rs):
    n, t, c = a.shape
    return a.reshape(n // rs, rs, t, c).transpose(0, 2, 1, 3).reshape(n // rs, t * rs, c)


def _deinterleave(a, rs, t):
    g, _, c = a.shape
    return a.reshape(g, t, rs, c).transpose(0, 2, 1, 3).reshape(g * rs, t, c)


def _trunk(x, p, conf_buf, dn_buf, s0, ffn_buf, w, *, interleaved):
    n, seq, _ = x.shape
    if interleaved:
        rs = SAMPLE_RS
        x, p, conf_buf, dn_buf, ffn_buf = (_interleave(a, rs) for a in (x, p, conf_buf, dn_buf, ffn_buf))
        R = seq * rs
    else:
        rs = 1
        R = 512
    conf, qkv, z, bg, conf_new, dn_new = _premix(
        x, conf_buf, dn_buf, w["g_mix"], w["w_in_main"], w["w_ba"], w["conf_w"], w["conf_b"],
        w["ln_g"], w["ln_b"], w["dn_w"], w["a_log"], w["dt_bias"], R=R, rs=rs)
    if interleaved:
        o, s_new = _delta_sample(qkv, z, bg, w["dn_norm_g"], s0, T=seq, rs=rs)
    else:
        o, s_new = _delta_prompt(qkv, z, bg, w["dn_norm_g"], s0, R=512)
    y, ffn_new = _postmix(x, conf, o, p, ffn_buf, w["w_out"], w["g_ffn"], w["w_up"], w["ffn_w"], w["ffn_b"],
                          w["w_down"], w["w_ple"], w["w_pg"], w["g_fin"], R=min(R, 256), rs=rs)
    if interleaved:
        y = _deinterleave(y, rs, seq)
        conf_new = _deinterleave(conf_new, rs, CONF_K - 1)
        dn_new = _deinterleave(dn_new, rs, DN_CONV_K - 1)
        ffn_new = _deinterleave(ffn_new, rs, FFN_CONV_K - 1)
    return y, conf_new[None], dn_new[None], s_new[None], ffn_new[None]


def kernel(x_prompt, x_sample, p_prompt, p_sample, state_conf_buf, state_dn_conv_buf, state_dn_S, state_ffn_buf,
           norm_mix_g, w_in, conf_dw_w, conf_dw_b, conf_ln_g, conf_ln_b, dn_conv_w, dn_a_log, dn_dt_bias,
           dn_norm_g, w_out, norm_ffn_g, w_up, ffn_conv_w, ffn_conv_b, w_down, w_ple, w_ple_gate, norm_final_g):
    assert w_in.shape[0] == 1, "single trunk layer"
    n_main = 2 * CONF_WIDTH + 4 * DN_WIDTH
    gate_pad = ((0, 0), (DN_HEADS, LANES - 2 * DN_HEADS))
    w = dict(
        g_mix=norm_mix_g[0][None], w_in_main=w_in[0][:, :n_main].astype(BF16),
        w_ba=jnp.pad(w_in[0][:, n_main:], ((0, 0), (0, LANES - 2 * DN_HEADS))).astype(BF16),
        conf_w=_sublane_replicated(conf_dw_w[0]), conf_b=conf_dw_b[0][None], ln_g=conf_ln_g[0][None], ln_b=conf_ln_b[0][None],
        dn_w=_sublane_replicated(dn_conv_w[0]), a_log=jnp.pad(dn_a_log[0][None], gate_pad), dt_bias=jnp.pad(dn_dt_bias[0][None], gate_pad),
        dn_norm_g=dn_norm_g[0][None], w_out=w_out[0].astype(BF16), g_ffn=norm_ffn_g[0][None],
        w_up=w_up[0].astype(BF16), ffn_w=_sublane_replicated(ffn_conv_w[0]), ffn_b=ffn_conv_b[0][None], w_down=w_down[0].astype(BF16),
        w_ple=w_ple[0].astype(BF16), w_pg=w_ple_gate[0].astype(BF16), g_fin=norm_final_g[None],
    )
    bp, dt = x_prompt.shape[0], x_prompt.dtype
    z_conf = jnp.zeros((bp, CONF_K - 1, CONF_WIDTH), dt)
    z_dn = jnp.zeros((bp, DN_CONV_K - 1, 3 * DN_WIDTH), dt)
    z_s = jnp.zeros((bp, DN_HEADS, DN_HEAD_DIM, DN_HEAD_DIM), dt)
    z_ffn = jnp.zeros((bp, FFN_CONV_K - 1, 2 * D_FF), dt)
    yp, pc, pd, ps, pf = _trunk(x_prompt, p_prompt[0], z_conf, z_dn, z_s, z_ffn, w, interleaved=False)
    ys, sc, sd, ss, sf = _trunk(x_sample, p_sample[0], state_conf_buf[0], state_dn_conv_buf[0], state_dn_S[0],
                                state_ffn_buf[0], w, interleaved=True)
    return (yp, ys, pc, pd, ps, pf, sc, sd, ss, sf)
```

```python
import functools

import jax
import jax.numpy as jnp
from jax import lax
from jax.experimental import pallas as pl
from jax.experimental.pallas import tpu as pltpu

F32 = jnp.float32
BF16 = jnp.bfloat16

D_MODEL = 1024
D_PLE = 256
CONF_WIDTH = 512
CONF_K = 31
DN_HEADS = 4
DN_HEAD_DIM = 128
DN_WIDTH = DN_HEADS * DN_HEAD_DIM
DN_CONV_K = 4
D_FF = 2816
FFN_CONV_K = 3
CHUNK = 64
EPS = 1e-6
N_MAIN = 2 * CONF_WIDTH + 4 * DN_WIDTH

SUBLANES = 8
LANES = 128
FF_COLS = 256
N_FF_CHUNKS = D_FF // FF_COLS
SAMPLE_RS = 32
PREMIX_ROWS = 512
DELTA_ROWS = 128
DELTA_SEQS = 8
POSTMIX_ROWS = 512
VMEM_LIMIT = 56 * 1024 * 1024
N_PRE_W = 10
N_POST_W = 9


def _round_up(n, m):
    return (n + m - 1) // m * m


def _mm(a, b):
    return jnp.dot(a.astype(BF16), b.astype(BF16), preferred_element_type=F32)


def _mm_nt(a, b):
    return lax.dot_general(a.astype(BF16), b.astype(BF16), (((1,), (1,)), ((), ())),
                           preferred_element_type=F32)


def _bmm(a, b):
    return jnp.einsum("nij,njk->nik", a.astype(BF16), b.astype(BF16), preferred_element_type=F32)


def _bmm_nt(a, b):
    return jnp.einsum("nid,njd->nij", a.astype(BF16), b.astype(BF16), preferred_element_type=F32)


def _bmm_tn(a, b):
    return jnp.einsum("nck,ncv->nkv", a.astype(BF16), b.astype(BF16), preferred_element_type=F32)


def _split3(x):
    hi = x.astype(BF16)
    r1 = x - hi.astype(F32)
    mid = r1.astype(BF16)
    lo = (r1 - mid.astype(F32)).astype(BF16)
    return hi, mid, lo


def _mm_exact_lhs(m_bf16, x):
    hi, mid, lo = _split3(x)
    dot = functools.partial(jnp.dot, preferred_element_type=F32)
    return dot(m_bf16, hi) + dot(m_bf16, mid) + dot(m_bf16, lo)


def _sigmoid(x):
    return 1.0 / (1.0 + jnp.exp(-x))


def _silu(x):
    return x * _sigmoid(x)


def _softplus(x):
    return jnp.maximum(x, 0.0) + jnp.log1p(jnp.exp(-jnp.abs(x)))


def _rms(x, g):
    return x * lax.rsqrt(jnp.mean(x * x, axis=-1, keepdims=True) + EPS) * g


def _halo(taps, rs):
    rows = (taps - 1) * rs
    return rows, _round_up(rows, SUBLANES)


def _halo_begin(ext, buf0_ref, taps, rs, R, t):
    rows, padded = _halo(taps, rs)

    @pl.when(t == 0)
    def _():
        ext[padded - rows:padded, :] = buf0_ref[...]

    @pl.when(t > 0)
    def _():
        ext[0:padded, :] = ext[R:R + padded, :]


def _halo_end(ext, buf_ref, taps, rs, R, t, NT):
    rows, padded = _halo(taps, rs)

    @pl.when(t == NT - 1)
    def _():
        buf_ref[...] = ext[padded + R - rows:padded + R, :]


def _causal_conv(ext_ref, w_ref, cols, taps, rs, r0, n_rows):
    _, padded = _halo(taps, rs)
    groups = {}
    for k in range(taps):
        off = padded - (taps - 1 - k) * rs
        groups.setdefault(off % SUBLANES, []).append((k, off - off % SUBLANES))
    total = None
    for mis, members in sorted(groups.items()):
        n = n_rows + (SUBLANES if mis else 0)
        part = None
        for k, base in members:
            w_rows = jnp.concatenate([w_ref[k, :, cols]] * (n // SUBLANES), axis=0)
            term = w_rows * ext_ref[base + r0:base + r0 + n, cols]
            part = term if part is None else part + term
        if mis:
            part = part[mis:mis + n_rows]
        total = part if total is None else total + part
    return total


def _premix_kernel(*refs, R, rs, NT):
    x_ref = refs[0]
    g_ref, win_ref, wba_ref, cw_ref, cb_ref, lng_ref, lnb_ref, dw_ref, alog_ref, dtb_ref = refs[1:1 + N_PRE_W]
    cbuf0_ref, dbuf0_ref, conf_ref, qkv_ref, z_ref, bg_ref, cbuf_ref, dbuf_ref, cext, dext = refs[1 + N_PRE_W:]
    t = pl.program_id(1)
    _, HC = _halo(CONF_K, rs)
    _, HD = _halo(DN_CONV_K, rs)
    q0 = 2 * CONF_WIDTH
    z0 = q0 + 3 * DN_WIDTH
    _halo_begin(cext, cbuf0_ref, CONF_K, rs, R, t)
    _halo_begin(dext, dbuf0_ref, DN_CONV_K, rs, R, t)

    h = _rms(x_ref[...], g_ref[...]).astype(BF16)
    vg = jnp.dot(h, win_ref[:, 0:q0], preferred_element_type=F32)
    cext[HC:HC + R, :] = vg[:, :CONF_WIDTH] * _sigmoid(vg[:, CONF_WIDTH:])
    dext[HD:HD + R, :] = jnp.dot(h, win_ref[:, q0:z0], preferred_element_type=F32)
    z_ref[...] = jnp.dot(h, win_ref[:, z0:N_MAIN], preferred_element_type=F32)

    ba = jnp.dot(h, wba_ref[...], preferred_element_type=F32)
    lane = lax.broadcasted_iota(jnp.int32, ba.shape, 1)
    gate = -jnp.exp(alog_ref[...]) * _softplus(ba + dtb_ref[...])
    bg_ref[...] = jnp.where(lane < DN_HEADS, _sigmoid(ba), gate)

    RC = 64
    for r0 in range(0, R, RC):
        acc = _causal_conv(cext, cw_ref, slice(None), CONF_K, rs, r0, RC) + cb_ref[...]
        mu = jnp.mean(acc, axis=-1, keepdims=True)
        cen = acc - mu
        var = jnp.mean(cen * cen, axis=-1, keepdims=True)
        n = cen * lax.rsqrt(var + EPS) * lng_ref[...] + lnb_ref[...]
        conf_ref[r0:r0 + RC, :] = _silu(n)

    RD = 32
    for r0 in range(0, R, RD):
        y = _silu(_causal_conv(dext, dw_ref, slice(None), DN_CONV_K, rs, r0, RD))
        for j in range(3 * DN_HEADS):
            seg = y[:, j * DN_HEAD_DIM:(j + 1) * DN_HEAD_DIM]
            if j < 2 * DN_HEADS:
                seg = seg * lax.rsqrt(jnp.sum(seg * seg, axis=-1, keepdims=True) + EPS)
            qkv_ref[r0:r0 + RD, j * DN_HEAD_DIM:(j + 1) * DN_HEAD_DIM] = seg

    _halo_end(cext, cbuf_ref, CONF_K, rs, R, t, NT)
    _halo_end(dext, dbuf_ref, DN_CONV_K, rs, R, t, NT)


def _premix(x, cbuf0, dbuf0, pre_w, *, R, rs):
    G, L, _ = x.shape
    NT = L // R
    c_halo, HC = _halo(CONF_K, rs)
    d_halo, HD = _halo(DN_CONV_K, rs)

    def tile(c):
        return pl.BlockSpec((None, R, c), lambda gi, ti: (gi, ti, 0))

    def full(a):
        return pl.BlockSpec(a.shape, lambda gi, ti: (0,) * a.ndim)

    def per_group(rows, c):
        return pl.BlockSpec((None, rows, c), lambda gi, ti: (gi, 0, 0))

    return pl.pallas_call(
        functools.partial(_premix_kernel, R=R, rs=rs, NT=NT),
        grid=(G, NT),
        in_specs=[tile(D_MODEL)] + [full(a) for a in pre_w]
        + [per_group(c_halo, CONF_WIDTH), per_group(d_halo, 3 * DN_WIDTH)],
        out_specs=[tile(CONF_WIDTH), tile(3 * DN_WIDTH), tile(DN_WIDTH), tile(LANES),
                   per_group(c_halo, CONF_WIDTH), per_group(d_halo, 3 * DN_WIDTH)],
        out_shape=[jax.ShapeDtypeStruct((G, L, CONF_WIDTH), F32),
                   jax.ShapeDtypeStruct((G, L, 3 * DN_WIDTH), F32),
                   jax.ShapeDtypeStruct((G, L, DN_WIDTH), F32),
                   jax.ShapeDtypeStruct((G, L, LANES), F32),
                   jax.ShapeDtypeStruct((G, c_halo, CONF_WIDTH), F32),
                   jax.ShapeDtypeStruct((G, d_halo, 3 * DN_WIDTH), F32)],
        scratch_shapes=[pltpu.VMEM((HC + R, CONF_WIDTH), F32), pltpu.VMEM((HD + R, 3 * DN_WIDTH), F32)],
        compiler_params=pltpu.CompilerParams(dimension_semantics=("arbitrary", "arbitrary"),
                                             vmem_limit_bytes=VMEM_LIMIT),
        name="premix",
    )(x, *pre_w, cbuf0, dbuf0)


def _neumann(a, eye, squarings, mm):
    x = eye - a
    p = a
    for _ in range(squarings):
        p = mm(p, p)
        x = x + mm(x, p)
    return x


def _head_norm_gate(o, z, ng):
    return o * lax.rsqrt(jnp.mean(o * o, axis=-1, keepdims=True) + EPS) * ng * _silu(z)


def _delta_prompt_kernel(qkv_ref, z_ref, bg_ref, ng_ref, s0_ref, o_ref, snew_ref, s_scr, *, SB, R, NT):
    t = pl.program_id(1)
    state_shape = (SB * DN_HEADS, DN_HEAD_DIM, DN_HEAD_DIM)

    @pl.when(t == 0)
    def _():
        s_scr[...] = s0_ref[...].reshape(state_shape)

    ri = lax.broadcasted_iota(jnp.int32, (R, R), 0)
    ci = lax.broadcasted_iota(jnp.int32, (R, R), 1)
    same_chunk = lax.shift_right_logical(ri, 6) == lax.shift_right_logical(ci, 6)
    csum = jnp.where(same_chunk & (ci <= ri), 1.0, 0.0).astype(BF16)
    bg = [bg_ref[b] for b in range(SB)]
    gc_all = [_mm_exact_lhs(csum, bg[b]) for b in range(SB)]
    gc_t = [g.T for g in gc_all]

    ri64 = lax.broadcasted_iota(jnp.int32, (CHUNK, CHUNK), 0)
    ci64 = lax.broadcasted_iota(jnp.int32, (CHUNK, CHUNK), 1)
    causal = ri64 >= ci64
    strict = ri64 > ci64
    eye = (ri64 == ci64).astype(F32)
    same_block = lax.shift_right_logical(ri64, 4) == lax.shift_right_logical(ci64, 4)
    scale = DN_HEAD_DIM ** -0.5
    NC = R // CHUNK
    per_chunk = SB * DN_HEADS
    blocks = [(b, c * CHUNK, h) for c in range(NC) for b in range(SB) for h in range(DN_HEADS)]

    def heads(col0):
        return jnp.stack([qkv_ref[b, r0:r0 + CHUNK, col0 + h * DN_HEAD_DIM:col0 + (h + 1) * DN_HEAD_DIM]
                          for b, r0, h in blocks])

    q = heads(0) * scale
    k = heads(DN_WIDTH)
    v = heads(2 * DN_WIDTH)
    beta = jnp.stack([bg[b][r0:r0 + CHUNK, h:h + 1] for b, r0, h in blocks])
    gcc = jnp.stack([gc_all[b][r0:r0 + CHUNK, DN_HEADS + h:DN_HEADS + h + 1] for b, r0, h in blocks])
    gcr = jnp.stack([gc_t[b][DN_HEADS + h:DN_HEADS + h + 1, r0:r0 + CHUNK] for b, r0, h in blocks])
    glast = gcc[:, CHUNK - 1:CHUNK, :]
    decay = jnp.exp(jnp.where(causal, gcc - gcr, -jnp.inf))
    kb = k * beta
    a = jnp.where(strict, _bmm_nt(kb, k) * decay, 0.0)
    a_diag = jnp.where(same_block, a, 0.0)
    a_off = jnp.where(same_block, 0.0, a)
    d = _neumann(a_diag, eye, 3, _bmm)
    b = _bmm(d, a_off)
    tinv = _bmm(_neumann(b, eye, 1, _bmm), d)
    eg = jnp.exp(gcc)
    uw = _bmm(tinv, jnp.concatenate([v * beta, kb * eg], axis=2))
    u = uw[:, :, :DN_HEAD_DIM]
    w = uw[:, :, DN_HEAD_DIM:]
    qk = jnp.where(causal, _bmm_nt(q, k) * decay, 0.0)
    qd = q * eg
    kd = k * jnp.exp(glast - gcc)
    eglast = jnp.exp(glast)

    s = s_scr[...]
    for c in range(NC):
        sl = slice(c * per_chunk, (c + 1) * per_chunk)
        v_new = u[sl] - _bmm(w[sl], s)
        o = _bmm(qd[sl], s) + _bmm(qk[sl], v_new)
        s = s * eglast[sl] + _bmm_tn(kd[sl], v_new)
        for n, (b, r0, h) in enumerate(blocks[sl]):
            lo = h * DN_HEAD_DIM
            o_ref[b, r0:r0 + CHUNK, lo:lo + DN_HEAD_DIM] = _head_norm_gate(
                o[n], z_ref[b, r0:r0 + CHUNK, lo:lo + DN_HEAD_DIM], ng_ref[...])
    s_scr[...] = s

    @pl.when(t == NT - 1)
    def _():
        snew_ref[...] = s_scr[...].reshape(snew_ref.shape)


def _delta_prompt(qkv, z, bg, ng, s0, *, SB, R):
    B, L, _ = qkv.shape
    NT = L // R

    def tile(c):
        return pl.BlockSpec((SB, R, c), lambda bi, ti: (bi, ti, 0))

    s_spec = pl.BlockSpec((SB, DN_HEADS, DN_HEAD_DIM, DN_HEAD_DIM), lambda bi, ti: (bi, 0, 0, 0))
    return pl.pallas_call(
        functools.partial(_delta_prompt_kernel, SB=SB, R=R, NT=NT),
        grid=(B // SB, NT),
        in_specs=[tile(3 * DN_WIDTH), tile(DN_WIDTH), tile(LANES),
                  pl.BlockSpec(ng.shape, lambda bi, ti: (0, 0)), s_spec],
        out_specs=[tile(DN_WIDTH), s_spec],
        out_shape=[jax.ShapeDtypeStruct((B, L, DN_WIDTH), F32), jax.ShapeDtypeStruct(s0.shape, F32)],
        scratch_shapes=[pltpu.VMEM((SB * DN_HEADS, DN_HEAD_DIM, DN_HEAD_DIM), F32)],
        compiler_params=pltpu.CompilerParams(dimension_semantics=("arbitrary", "arbitrary"),
                                             vmem_limit_bytes=VMEM_LIMIT),
        name="delta_prompt",
    )(qkv, z, bg, ng, s0)


def _delta_sample_kernel(qkv_ref, z_ref, bg_ref, ng_ref, s0_ref, o_ref, snew_ref,
                         w_scr, qd_scr, kd_scr, u_scr, vn_scr, os_scr, *, T, rs):
    R = T * rs
    shift = rs.bit_length() - 1
    bg = bg_ref[...]
    ri = lax.broadcasted_iota(jnp.int32, (R, R), 0)
    ci = lax.broadcasted_iota(jnp.int32, (R, R), 1)
    same_seq = (ri & (rs - 1)) == (ci & (rs - 1))
    ti = lax.shift_right_logical(ri, shift)
    tj = lax.shift_right_logical(ci, shift)
    causal = same_seq & (ti >= tj)
    strict = same_seq & (ti > tj)
    eye = (ri == ci).astype(F32)
    gc_all = _mm_exact_lhs(jnp.where(causal, 1.0, 0.0).astype(BF16), bg)
    gc_t = gc_all.T
    scale = DN_HEAD_DIM ** -0.5
    squarings = (T - 1).bit_length() - 1
    seq_rows = [pl.ds(s, T, stride=rs) for s in range(rs)]

    for h in range(DN_HEADS):
        lo = h * DN_HEAD_DIM
        q = qkv_ref[:, lo:lo + DN_HEAD_DIM] * scale
        k = qkv_ref[:, DN_WIDTH + lo:DN_WIDTH + lo + DN_HEAD_DIM]
        v = qkv_ref[:, 2 * DN_WIDTH + lo:2 * DN_WIDTH + lo + DN_HEAD_DIM]
        beta = bg[:, h:h + 1]
        gcc = gc_all[:, DN_HEADS + h:DN_HEADS + h + 1]
        gcr = gc_t[DN_HEADS + h:DN_HEADS + h + 1, :]
        glast = gc_all[R - rs:R, DN_HEADS + h:DN_HEADS + h + 1]
        glast_rows = jnp.concatenate([glast] * T, axis=0)
        decay = jnp.exp(jnp.where(causal, gcc - gcr, -jnp.inf))
        kb = k * beta
        a = jnp.where(strict, _mm_nt(kb, k) * decay, 0.0)
        tinv = _neumann(a, eye, squarings, _mm)
        eg = jnp.exp(gcc)
        uw = _mm(tinv, jnp.concatenate([v * beta, kb * eg], axis=1))
        u_scr[...] = uw[:, :DN_HEAD_DIM]
        w_scr[...] = uw[:, DN_HEAD_DIM:]
        qd_scr[...] = q * eg
        kd_scr[...] = k * jnp.exp(glast_rows - gcc)
        qk = jnp.where(causal, _mm_nt(q, k) * decay, 0.0)
        eglast = jnp.exp(glast)
        wq = jnp.stack([jnp.concatenate([w_scr[rows, :], qd_scr[rows, :]], axis=0) for rows in seq_rows])
        us = jnp.stack([u_scr[rows, :] for rows in seq_rows])
        kds = jnp.stack([kd_scr[rows, :] for rows in seq_rows])
        egl = jnp.stack([eglast[s:s + 1, :] for s in range(rs)])
        state = s0_ref[:, h]
        ws = _bmm(wq, state)
        v_new = us - ws[:, :T]
        snew_ref[:, h] = state * egl + _bmm_tn(kds, v_new)
        for s, rows in enumerate(seq_rows):
            vn_scr[rows, :] = v_new[s]
            os_scr[rows, :] = ws[s, T:]
        o = os_scr[...] + _mm(qk, vn_scr[...])
        o_ref[:, lo:lo + DN_HEAD_DIM] = _head_norm_gate(o, z_ref[:, lo:lo + DN_HEAD_DIM], ng_ref[...])


def _delta_sample(qkv, z, bg, ng, s0, *, T, rs):
    G, R, _ = qkv.shape

    def tile(c):
        return pl.BlockSpec((None, R, c), lambda gi: (gi, 0, 0))

    s_spec = pl.BlockSpec((rs, DN_HEADS, DN_HEAD_DIM, DN_HEAD_DIM), lambda gi: (gi, 0, 0, 0))
    return pl.pallas_call(
        functools.partial(_delta_sample_kernel, T=T, rs=rs),
        grid=(G,),
        in_specs=[tile(3 * DN_WIDTH), tile(DN_WIDTH), tile(LANES),
                  pl.BlockSpec(ng.shape, lambda gi: (0, 0)), s_spec],
        out_specs=[tile(DN_WIDTH), s_spec],
        out_shape=[jax.ShapeDtypeStruct((G, R, DN_WIDTH), F32), jax.ShapeDtypeStruct(s0.shape, F32)],
        scratch_shapes=[pltpu.VMEM((R, DN_HEAD_DIM), F32) for _ in range(6)],
        compiler_params=pltpu.CompilerParams(dimension_semantics=("arbitrary",),
                                             vmem_limit_bytes=VMEM_LIMIT),
        name="delta_sample",
    )(qkv, z, bg, ng, s0)


def _postmix_kernel(*refs, R, rs, NT):
    x_ref, conf_ref, o_ref, p_ref, fbuf0_ref = refs[:5]
    wout_ref, gffn_ref, wup_ref, fw_ref, fb_ref, wdown_ref, wple_ref, wpg_ref, gfin_ref = refs[5:5 + N_POST_W]
    y_ref, fbuf_ref, fext, act_scr = refs[5 + N_POST_W:]
    t = pl.program_id(1)
    _, HF = _halo(FFN_CONV_K, rs)
    _halo_begin(fext, fbuf0_ref, FFN_CONV_K, rs, R, t)

    mix = jnp.concatenate([conf_ref[...].astype(BF16), o_ref[...].astype(BF16)], axis=1)
    x1 = x_ref[...] + jnp.dot(mix, wout_ref[...], preferred_element_type=F32)
    hn = _rms(x1, gffn_ref[...]).astype(BF16)
    for j in range(2 * N_FF_CHUNKS):
        c0 = j * FF_COLS
        fext[HF:HF + R, c0:c0 + FF_COLS] = jnp.dot(hn, wup_ref[:, c0:c0 + FF_COLS],
                                                   preferred_element_type=F32)

    RC = 64
    for j in range(N_FF_CHUNKS):
        for r0 in range(0, R, RC):
            halves = []
            for c0 in (j * FF_COLS, D_FF + j * FF_COLS):
                cols = slice(c0, c0 + FF_COLS)
                halves.append(_causal_conv(fext, fw_ref, cols, FFN_CONV_K, rs, r0, RC) + fb_ref[:, cols])
            act_scr[r0:r0 + RC, j * FF_COLS:(j + 1) * FF_COLS] = (_silu(halves[0]) * halves[1]).astype(BF16)

    x2 = x1 + jnp.dot(act_scr[...], wdown_ref[...], preferred_element_type=F32)
    ple = jnp.dot(p_ref[...].astype(BF16), wple_ref[...], preferred_element_type=F32)
    x3 = x2 + ple * _sigmoid(jnp.dot(x2.astype(BF16), wpg_ref[...], preferred_element_type=F32))
    y_ref[...] = _rms(x3, gfin_ref[...])

    _halo_end(fext, fbuf_ref, FFN_CONV_K, rs, R, t, NT)


def _postmix(x, conf, o, p, fbuf0, post_w, *, R, rs):
    G, L, _ = x.shape
    NT = L // R
    halo, HF = _halo(FFN_CONV_K, rs)

    def tile(c):
        return pl.BlockSpec((None, R, c), lambda gi, ti: (gi, ti, 0))

    def full(a):
        return pl.BlockSpec(a.shape, lambda gi, ti: (0,) * a.ndim)

    buf_spec = pl.BlockSpec((None, halo, 2 * D_FF), lambda gi, ti: (gi, 0, 0))
    return pl.pallas_call(
        functools.partial(_postmix_kernel, R=R, rs=rs, NT=NT),
        grid=(G, NT),
        in_specs=[tile(D_MODEL), tile(CONF_WIDTH), tile(DN_WIDTH), tile(D_PLE), buf_spec]
        + [full(a) for a in post_w],
        out_specs=[tile(D_MODEL), buf_spec],
        out_shape=[jax.ShapeDtypeStruct((G, L, D_MODEL), F32),
                   jax.ShapeDtypeStruct((G, halo, 2 * D_FF), F32)],
        scratch_shapes=[pltpu.VMEM((HF + R, 2 * D_FF), F32), pltpu.VMEM((R, D_FF), BF16)],
        compiler_params=pltpu.CompilerParams(dimension_semantics=("arbitrary", "arbitrary"),
                                             vmem_limit_bytes=VMEM_LIMIT),
        name="postmix",
    )(x, conf, o, p, fbuf0, *post_w)


def _sublane_replicated(w):
    return jnp.broadcast_to(w[:, None, :], (w.shape[0], SUBLANES, w.shape[1]))


def _interleave(a, rs):
    n, t, c = a.shape
    return a.reshape(n // rs, rs, t, c).transpose(0, 2, 1, 3).reshape(n // rs, t * rs, c)


def _deinterleave(a, rs, t):
    g, _, c = a.shape
    return a.reshape(g, t, rs, c).transpose(0, 2, 1, 3).reshape(g * rs, t, c)


def _trunk(x, p, conf_buf, dn_buf, s0, ffn_buf, pre_w, ng, post_w, *, interleaved):
    seq = x.shape[1]
    if interleaved:
        rs = SAMPLE_RS
        x, p, conf_buf, dn_buf, ffn_buf = (_interleave(a, rs) for a in (x, p, conf_buf, dn_buf, ffn_buf))
        pre_rows = post_rows = seq * rs
    else:
        rs = 1
        pre_rows, post_rows = PREMIX_ROWS, POSTMIX_ROWS
    conf, qkv, z, bg, conf_new, dn_new = _premix(x, conf_buf, dn_buf, pre_w, R=pre_rows, rs=rs)
    if interleaved:
        o, s_new = _delta_sample(qkv, z, bg, ng, s0, T=seq, rs=rs)
    else:
        o, s_new = _delta_prompt(qkv, z, bg, ng, s0, SB=DELTA_SEQS, R=DELTA_ROWS)
    y, ffn_new = _postmix(x, conf, o, p, ffn_buf, post_w, R=post_rows, rs=rs)
    if interleaved:
        y = _deinterleave(y, rs, seq)
        conf_new = _deinterleave(conf_new, rs, CONF_K - 1)
        dn_new = _deinterleave(dn_new, rs, DN_CONV_K - 1)
        ffn_new = _deinterleave(ffn_new, rs, FFN_CONV_K - 1)
    return y, conf_new[None], dn_new[None], s_new[None], ffn_new[None]


def kernel(x_prompt, x_sample, p_prompt, p_sample, state_conf_buf, state_dn_conv_buf, state_dn_S, state_ffn_buf,
           norm_mix_g, w_in, conf_dw_w, conf_dw_b, conf_ln_g, conf_ln_b, dn_conv_w, dn_a_log, dn_dt_bias,
           dn_norm_g, w_out, norm_ffn_g, w_up, ffn_conv_w, ffn_conv_b, w_down, w_ple, w_ple_gate, norm_final_g):
    assert w_in.shape[0] == 1, "single trunk layer"
    gate_pad = ((0, 0), (DN_HEADS, LANES - 2 * DN_HEADS))
    pre_w = (norm_mix_g[0][None], w_in[0].astype(BF16),
             jnp.pad(w_in[0][:, N_MAIN:], ((0, 0), (0, LANES - 2 * DN_HEADS))).astype(BF16),
             _sublane_replicated(conf_dw_w[0]), conf_dw_b[0][None], conf_ln_g[0][None], conf_ln_b[0][None],
             _sublane_replicated(dn_conv_w[0]), jnp.pad(dn_a_log[0][None], gate_pad),
             jnp.pad(dn_dt_bias[0][None], gate_pad))
    ng = dn_norm_g[0][None]
    post_w = (w_out[0].astype(BF16), norm_ffn_g[0][None], w_up[0].astype(BF16),
              _sublane_replicated(ffn_conv_w[0]), ffn_conv_b[0][None], w_down[0].astype(BF16),
              w_ple[0].astype(BF16), w_ple_gate[0].astype(BF16), norm_final_g[None])
    assert len(pre_w) == N_PRE_W and len(post_w) == N_POST_W

    bp, dt = x_prompt.shape[0], x_prompt.dtype
    z_conf = jnp.zeros((bp, CONF_K - 1, CONF_WIDTH), dt)
    z_dn = jnp.zeros((bp, DN_CONV_K - 1, 3 * DN_WIDTH), dt)
    z_s = jnp.zeros((bp, DN_HEADS, DN_HEAD_DIM, DN_HEAD_DIM), dt)
    z_ffn = jnp.zeros((bp, FFN_CONV_K - 1, 2 * D_FF), dt)
    yp, pc, pd, ps, pf = _trunk(x_prompt, p_prompt[0], z_conf, z_dn, z_s, z_ffn, pre_w, ng, post_w,
                                interleaved=False)
    ys, sc, sd, ss, sf = _trunk(x_sample, p_sample[0], state_conf_buf[0], state_dn_conv_buf[0], state_dn_S[0],
                                state_ffn_buf[0], pre_w, ng, post_w, interleaved=True)
    return (yp, ys, pc, pd, ps, pf, sc, sd, ss, sf)
```

```python
import functools

import jax
import jax.numpy as jnp
from jax import lax
from jax.experimental import pallas as pl
from jax.experimental.pallas import tpu as pltpu

F32 = jnp.float32
BF16 = jnp.bfloat16

D_MODEL = 1024
D_PLE = 256
CONF_WIDTH = 512
CONF_K = 31
DN_HEADS = 4
DN_HEAD_DIM = 128
DN_WIDTH = DN_HEADS * DN_HEAD_DIM
DN_CONV_K = 4
D_FF = 2816
FFN_CONV_K = 3
CHUNK = 64
EPS = 1e-6
N_MAIN = 2 * CONF_WIDTH + 4 * DN_WIDTH

SUBLANES = 8
LANES = 128
FF_COLS = 256
N_FF_CHUNKS = D_FF // FF_COLS
SAMPLE_RS = 32
PREMIX_ROWS = 512
DELTA_ROWS = 128
DELTA_SEQS = 8
POSTMIX_ROWS = 512
VMEM_LIMIT = 56 * 1024 * 1024
N_PRE_W = 10
N_POST_W = 9


def _round_up(n, m):
    return (n + m - 1) // m * m


def _mm(a, b):
    return jnp.dot(a.astype(BF16), b.astype(BF16), preferred_element_type=F32)


def _mm_nt(a, b):
    return lax.dot_general(a.astype(BF16), b.astype(BF16), (((1,), (1,)), ((), ())),
                           preferred_element_type=F32)


def _bmm(a, b):
    return jnp.einsum("nij,njk->nik", a.astype(BF16), b.astype(BF16), preferred_element_type=F32)


def _bmm_nt(a, b):
    return jnp.einsum("nid,njd->nij", a.astype(BF16), b.astype(BF16), preferred_element_type=F32)


def _bmm_tn(a, b):
    return jnp.einsum("nck,ncv->nkv", a.astype(BF16), b.astype(BF16), preferred_element_type=F32)


def _split3(x):
    hi = x.astype(BF16)
    r1 = x - hi.astype(F32)
    mid = r1.astype(BF16)
    lo = (r1 - mid.astype(F32)).astype(BF16)
    return hi, mid, lo


def _mm_exact_lhs(m_bf16, x):
    hi, mid, lo = _split3(x)
    dot = functools.partial(jnp.dot, preferred_element_type=F32)
    return dot(m_bf16, hi) + dot(m_bf16, mid) + dot(m_bf16, lo)


def _sigmoid(x):
    return 1.0 / (1.0 + jnp.exp(-x))


def _silu(x):
    return x * _sigmoid(x)


def _softplus(x):
    return jnp.maximum(x, 0.0) + jnp.log1p(jnp.exp(-jnp.abs(x)))


def _rms(x, g):
    return x * lax.rsqrt(jnp.mean(x * x, axis=-1, keepdims=True) + EPS) * g


def _halo(taps, rs):
    rows = (taps - 1) * rs
    return rows, _round_up(rows, SUBLANES)


def _halo_begin(ext, buf0_ref, taps, rs, R, t):
    rows, padded = _halo(taps, rs)

    @pl.when(t == 0)
    def _():
        ext[padded - rows:padded, :] = buf0_ref[...]

    @pl.when(t > 0)
    def _():
        ext[0:padded, :] = ext[R:R + padded, :]


def _halo_end(ext, buf_ref, taps, rs, R, t, NT):
    rows, padded = _halo(taps, rs)

    @pl.when(t == NT - 1)
    def _():
        buf_ref[...] = ext[padded + R - rows:padded + R, :]


def _causal_conv(ext_ref, w_ref, cols, taps, rs, r0, n_rows):
    _, padded = _halo(taps, rs)
    groups = {}
    for k in range(taps):
        off = padded - (taps - 1 - k) * rs
        groups.setdefault(off % SUBLANES, []).append((k, off - off % SUBLANES))
    total = None
    for mis, members in sorted(groups.items()):
        n = n_rows + (SUBLANES if mis else 0)
        part = None
        for k, base in members:
            w_rows = jnp.concatenate([w_ref[k, :, cols]] * (n // SUBLANES), axis=0)
            term = w_rows * ext_ref[base + r0:base + r0 + n, cols]
            part = term if part is None else part + term
        if mis:
            part = part[mis:mis + n_rows]
        total = part if total is None else total + part
    return total


def _premix_kernel(*refs, R, rs, NT):
    x_ref = refs[0]
    g_ref, win_ref, wba_ref, cw_ref, cb_ref, lng_ref, lnb_ref, dw_ref, alog_ref, dtb_ref = refs[1:1 + N_PRE_W]
    cbuf0_ref, dbuf0_ref, conf_ref, qkv_ref, z_ref, bg_ref, cbuf_ref, dbuf_ref, cext, dext = refs[1 + N_PRE_W:]
    t = pl.program_id(1)
    _, HC = _halo(CONF_K, rs)
    _, HD = _halo(DN_CONV_K, rs)
    q0 = 2 * CONF_WIDTH
    z0 = q0 + 3 * DN_WIDTH
    _halo_begin(cext, cbuf0_ref, CONF_K, rs, R, t)
    _halo_begin(dext, dbuf0_ref, DN_CONV_K, rs, R, t)

    h = _rms(x_ref[...], g_ref[...]).astype(BF16)
    RG = 128
    for r0 in range(0, R, RG):
        vg = jnp.dot(h[r0:r0 + RG], win_ref[:, 0:q0], preferred_element_type=F32)
        cext[HC + r0:HC + r0 + RG, :] = vg[:, :CONF_WIDTH] * _sigmoid(vg[:, CONF_WIDTH:])

    def zero_row_after(piece):
        bits = pltpu.bitcast(piece[0:SUBLANES, 0:LANES], jnp.uint32)
        zero = pltpu.bitcast(lax.shift_right_logical(lax.shift_right_logical(bits, jnp.uint32(16)), jnp.uint32(16)), F32)
        return jnp.concatenate([zero[0:1, :]] * (CONF_WIDTH // LANES), axis=1)

    ordering_zeros = []
    for c0 in range(q0, N_MAIN, FF_COLS):
        piece = jnp.dot(h, win_ref[:, c0:c0 + FF_COLS], preferred_element_type=F32)
        if c0 < z0:
            dext[HD:HD + R, c0 - q0:c0 - q0 + FF_COLS] = piece
        else:
            z_ref[:, c0 - z0:c0 - z0 + FF_COLS] = piece
        ordering_zeros.append(zero_row_after(piece))

    ba = jnp.dot(h, wba_ref[...], preferred_element_type=F32)
    lane = lax.broadcasted_iota(jnp.int32, ba.shape, 1)
    gate = -jnp.exp(alog_ref[...]) * _softplus(ba + dtb_ref[...])
    bg_ref[...] = jnp.where(lane < DN_HEADS, _sigmoid(ba), gate)

    RC = 64
    for r0 in range(0, R, RC):
        bias = cb_ref[...] + ordering_zeros[min(r0 // RC * len(ordering_zeros) // (R // RC), len(ordering_zeros) - 1)]
        acc = _causal_conv(cext, cw_ref, slice(None), CONF_K, rs, r0, RC) + bias
        mu = jnp.mean(acc, axis=-1, keepdims=True)
        cen = acc - mu
        var = jnp.mean(cen * cen, axis=-1, keepdims=True)
        n = cen * lax.rsqrt(var + EPS) * lng_ref[...] + lnb_ref[...]
        conf_ref[r0:r0 + RC, :] = _silu(n)

    RD = 32
    for r0 in range(0, R, RD):
        y = _silu(_causal_conv(dext, dw_ref, slice(None), DN_CONV_K, rs, r0, RD))
        for j in range(3 * DN_HEADS):
            seg = y[:, j * DN_HEAD_DIM:(j + 1) * DN_HEAD_DIM]
            if j < 2 * DN_HEADS:
                seg = seg * lax.rsqrt(jnp.sum(seg * seg, axis=-1, keepdims=True) + EPS)
            qkv_ref[r0:r0 + RD, j * DN_HEAD_DIM:(j + 1) * DN_HEAD_DIM] = seg

    _halo_end(cext, cbuf_ref, CONF_K, rs, R, t, NT)
    _halo_end(dext, dbuf_ref, DN_CONV_K, rs, R, t, NT)


def _premix(x, cbuf0, dbuf0, pre_w, *, R, rs):
    G, L, _ = x.shape
    NT = L // R
    c_halo, HC = _halo(CONF_K, rs)
    d_halo, HD = _halo(DN_CONV_K, rs)

    def tile(c):
        return pl.BlockSpec((None, R, c), lambda gi, ti: (gi, ti, 0))

    def full(a):
        return pl.BlockSpec(a.shape, lambda gi, ti: (0,) * a.ndim)

    def per_group(rows, c):
        return pl.BlockSpec((None, rows, c), lambda gi, ti: (gi, 0, 0))

    return pl.pallas_call(
        functools.partial(_premix_kernel, R=R, rs=rs, NT=NT),
        grid=(G, NT),
        in_specs=[tile(D_MODEL)] + [full(a) for a in pre_w]
        + [per_group(c_halo, CONF_WIDTH), per_group(d_halo, 3 * DN_WIDTH)],
        out_specs=[tile(CONF_WIDTH), tile(3 * DN_WIDTH), tile(DN_WIDTH), tile(LANES),
                   per_group(c_halo, CONF_WIDTH), per_group(d_halo, 3 * DN_WIDTH)],
        out_shape=[jax.ShapeDtypeStruct((G, L, CONF_WIDTH), F32),
                   jax.ShapeDtypeStruct((G, L, 3 * DN_WIDTH), F32),
                   jax.ShapeDtypeStruct((G, L, DN_WIDTH), F32),
                   jax.ShapeDtypeStruct((G, L, LANES), F32),
                   jax.ShapeDtypeStruct((G, c_halo, CONF_WIDTH), F32),
                   jax.ShapeDtypeStruct((G, d_halo, 3 * DN_WIDTH), F32)],
        scratch_shapes=[pltpu.VMEM((HC + R, CONF_WIDTH), F32), pltpu.VMEM((HD + R, 3 * DN_WIDTH), F32)],
        compiler_params=pltpu.CompilerParams(dimension_semantics=("arbitrary", "arbitrary"),
                                             vmem_limit_bytes=VMEM_LIMIT),
        name="premix",
    )(x, *pre_w, cbuf0, dbuf0)


def _neumann(a, eye, squarings, mm):
    x = eye - a
    p = a
    for _ in range(squarings):
        p = mm(p, p)
        x = x + mm(x, p)
    return x


def _head_norm_gate(o, z, ng):
    return o * lax.rsqrt(jnp.mean(o * o, axis=-1, keepdims=True) + EPS) * ng * _silu(z)


def _delta_prompt_kernel(qkv_ref, z_ref, bg_ref, ng_ref, s0_ref, o_ref, snew_ref, s_scr, *, SB, R, NT):
    t = pl.program_id(1)
    state_shape = (SB * DN_HEADS, DN_HEAD_DIM, DN_HEAD_DIM)

    @pl.when(t == 0)
    def _():
        s_scr[...] = s0_ref[...].reshape(state_shape)

    ri = lax.broadcasted_iota(jnp.int32, (R, R), 0)
    ci = lax.broadcasted_iota(jnp.int32, (R, R), 1)
    same_chunk = lax.shift_right_logical(ri, 6) == lax.shift_right_logical(ci, 6)
    csum = jnp.where(same_chunk & (ci <= ri), 1.0, 0.0).astype(BF16)
    bg = [bg_ref[b] for b in range(SB)]
    gc_all = [_mm_exact_lhs(csum, bg[b]) for b in range(SB)]
    gc_t = [g.T for g in gc_all]

    ri64 = lax.broadcasted_iota(jnp.int32, (CHUNK, CHUNK), 0)
    ci64 = lax.broadcasted_iota(jnp.int32, (CHUNK, CHUNK), 1)
    causal = ri64 >= ci64
    strict = ri64 > ci64
    eye = (ri64 == ci64).astype(F32)
    same_block = lax.shift_right_logical(ri64, 4) == lax.shift_right_logical(ci64, 4)
    scale = DN_HEAD_DIM ** -0.5
    NC = R // CHUNK
    per_chunk = SB * DN_HEADS
    blocks = [(b, c * CHUNK, h) for c in range(NC) for b in range(SB) for h in range(DN_HEADS)]

    def heads(col0):
        return jnp.stack([qkv_ref[b, r0:r0 + CHUNK, col0 + h * DN_HEAD_DIM:col0 + (h + 1) * DN_HEAD_DIM]
                          for b, r0, h in blocks])

    q = heads(0) * scale
    k = heads(DN_WIDTH)
    v = heads(2 * DN_WIDTH)
    beta = jnp.stack([bg[b][r0:r0 + CHUNK, h:h + 1] for b, r0, h in blocks])
    gcc = jnp.stack([gc_all[b][r0:r0 + CHUNK, DN_HEADS + h:DN_HEADS + h + 1] for b, r0, h in blocks])
    gcr = jnp.stack([gc_t[b][DN_HEADS + h:DN_HEADS + h + 1, r0:r0 + CHUNK] for b, r0, h in blocks])
    glast = gcc[:, CHUNK - 1:CHUNK, :]
    decay = jnp.exp(jnp.where(causal, gcc - gcr, -jnp.inf))
    kb = k * beta
    a = jnp.where(strict, _bmm_nt(kb, k) * decay, 0.0)
    a_diag = jnp.where(same_block, a, 0.0)
    a_off = jnp.where(same_block, 0.0, a)
    d = _neumann(a_diag, eye, 3, _bmm)
    b = _bmm(d, a_off)
    tinv = _bmm(_neumann(b, eye, 1, _bmm), d)
    eg = jnp.exp(gcc)
    uw = _bmm(tinv, jnp.concatenate([v * beta, kb * eg], axis=2))
    u = uw[:, :, :DN_HEAD_DIM]
    w = uw[:, :, DN_HEAD_DIM:]
    qk = jnp.where(causal, _bmm_nt(q, k) * decay, 0.0)
    qd = q * eg
    kd = k * jnp.exp(glast - gcc)
    eglast = jnp.exp(glast)

    s = s_scr[...]
    for c in range(NC):
        sl = slice(c * per_chunk, (c + 1) * per_chunk)
        v_new = u[sl] - _bmm(w[sl], s)
        o = _bmm(qd[sl], s) + _bmm(qk[sl], v_new)
        s = s * eglast[sl] + _bmm_tn(kd[sl], v_new)
        for n, (b, r0, h) in enumerate(blocks[sl]):
            lo = h * DN_HEAD_DIM
            o_ref[b, r0:r0 + CHUNK, lo:lo + DN_HEAD_DIM] = _head_norm_gate(
                o[n], z_ref[b, r0:r0 + CHUNK, lo:lo + DN_HEAD_DIM], ng_ref[...])
    s_scr[...] = s

    @pl.when(t == NT - 1)
    def _():
        snew_ref[...] = s_scr[...].reshape(snew_ref.shape)


def _delta_prompt(qkv, z, bg, ng, s0, *, SB, R):
    B, L, _ = qkv.shape
    NT = L // R

    def tile(c):
        return pl.BlockSpec((SB, R, c), lambda bi, ti: (bi, ti, 0))

    s_spec = pl.BlockSpec((SB, DN_HEADS, DN_HEAD_DIM, DN_HEAD_DIM), lambda bi, ti: (bi, 0, 0, 0))
    return pl.pallas_call(
        functools.partial(_delta_prompt_kernel, SB=SB, R=R, NT=NT),
        grid=(B // SB, NT),
        in_specs=[tile(3 * DN_WIDTH), tile(DN_WIDTH), tile(LANES),
                  pl.BlockSpec(ng.shape, lambda bi, ti: (0, 0)), s_spec],
        out_specs=[tile(DN_WIDTH), s_spec],
        out_shape=[jax.ShapeDtypeStruct((B, L, DN_WIDTH), F32), jax.ShapeDtypeStruct(s0.shape, F32)],
        scratch_shapes=[pltpu.VMEM((SB * DN_HEADS, DN_HEAD_DIM, DN_HEAD_DIM), F32)],
        compiler_params=pltpu.CompilerParams(dimension_semantics=("arbitrary", "arbitrary"),
                                             vmem_limit_bytes=VMEM_LIMIT),
        name="delta_prompt",
    )(qkv, z, bg, ng, s0)


def _delta_sample_kernel(qkv_ref, z_ref, bg_ref, ng_ref, s0_ref, o_ref, snew_ref,
                         w_scr, qd_scr, kd_scr, u_scr, vn_scr, os_scr, *, T, rs):
    R = T * rs
    shift = rs.bit_length() - 1
    bg = bg_ref[...]
    ri = lax.broadcasted_iota(jnp.int32, (R, R), 0)
    ci = lax.broadcasted_iota(jnp.int32, (R, R), 1)
    same_seq = (ri & (rs - 1)) == (ci & (rs - 1))
    ti = lax.shift_right_logical(ri, shift)
    tj = lax.shift_right_logical(ci, shift)
    causal = same_seq & (ti >= tj)
    strict = same_seq & (ti > tj)
    eye = (ri == ci).astype(F32)
    gc_all = _mm_exact_lhs(jnp.where(causal, 1.0, 0.0).astype(BF16), bg)
    gc_t = gc_all.T
    scale = DN_HEAD_DIM ** -0.5
    squarings = (T - 1).bit_length() - 1
    seq_rows = [pl.ds(s, T, stride=rs) for s in range(rs)]

    for h in range(DN_HEADS):
        lo = h * DN_HEAD_DIM
        q = qkv_ref[:, lo:lo + DN_HEAD_DIM] * scale
        k = qkv_ref[:, DN_WIDTH + lo:DN_WIDTH + lo + DN_HEAD_DIM]
        v = qkv_ref[:, 2 * DN_WIDTH + lo:2 * DN_WIDTH + lo + DN_HEAD_DIM]
        beta = bg[:, h:h + 1]
        gcc = gc_all[:, DN_HEADS + h:DN_HEADS + h + 1]
        gcr = gc_t[DN_HEADS + h:DN_HEADS + h + 1, :]
        glast = gc_all[R - rs:R, DN_HEADS + h:DN_HEADS + h + 1]
        glast_rows = jnp.concatenate([glast] * T, axis=0)
        decay = jnp.exp(jnp.where(causal, gcc - gcr, -jnp.inf))
        kb = k * beta
        a = jnp.where(strict, _mm_nt(kb, k) * decay, 0.0)
        tinv = _neumann(a, eye, squarings, _mm)
        eg = jnp.exp(gcc)
        uw = _mm(tinv, jnp.concatenate([v * beta, kb * eg], axis=1))
        u_scr[...] = uw[:, :DN_HEAD_DIM]
        w_scr[...] = uw[:, DN_HEAD_DIM:]
        qd_scr[...] = q * eg
        kd_scr[...] = k * jnp.exp(glast_rows - gcc)
        qk = jnp.where(causal, _mm_nt(q, k) * decay, 0.0)
        eglast = jnp.exp(glast)
        wq = jnp.stack([jnp.concatenate([w_scr[rows, :], qd_scr[rows, :]], axis=0) for rows in seq_rows])
        us = jnp.stack([u_scr[rows, :] for rows in seq_rows])
        kds = jnp.stack([kd_scr[rows, :] for rows in seq_rows])
        egl = jnp.stack([eglast[s:s + 1, :] for s in range(rs)])
        state = s0_ref[:, h]
        ws = _bmm(wq, state)
        v_new = us - ws[:, :T]
        snew_ref[:, h] = state * egl + _bmm_tn(kds, v_new)
        for s, rows in enumerate(seq_rows):
            vn_scr[rows, :] = v_new[s]
            os_scr[rows, :] = ws[s, T:]
        o = os_scr[...] + _mm(qk, vn_scr[...])
        o_ref[:, lo:lo + DN_HEAD_DIM] = _head_norm_gate(o, z_ref[:, lo:lo + DN_HEAD_DIM], ng_ref[...])


def _delta_sample(qkv, z, bg, ng, s0, *, T, rs):
    G, R, _ = qkv.shape

    def tile(c):
        return pl.BlockSpec((None, R, c), lambda gi: (gi, 0, 0))

    s_spec = pl.BlockSpec((rs, DN_HEADS, DN_HEAD_DIM, DN_HEAD_DIM), lambda gi: (gi, 0, 0, 0))
    return pl.pallas_call(
        functools.partial(_delta_sample_kernel, T=T, rs=rs),
        grid=(G,),
        in_specs=[tile(3 * DN_WIDTH), tile(DN_WIDTH), tile(LANES),
                  pl.BlockSpec(ng.shape, lambda gi: (0, 0)), s_spec],
        out_specs=[tile(DN_WIDTH), s_spec],
        out_shape=[jax.ShapeDtypeStruct((G, R, DN_WIDTH), F32), jax.ShapeDtypeStruct(s0.shape, F32)],
        scratch_shapes=[pltpu.VMEM((R, DN_HEAD_DIM), F32) for _ in range(6)],
        compiler_params=pltpu.CompilerParams(dimension_semantics=("arbitrary",),
                                             vmem_limit_bytes=VMEM_LIMIT),
        name="delta_sample",
    )(qkv, z, bg, ng, s0)


def _postmix_kernel(*refs, R, rs, NT):
    x_ref, conf_ref, o_ref, p_ref, fbuf0_ref = refs[:5]
    wout_ref, gffn_ref, wup_ref, fw_ref, fb_ref, wdown_ref, wple_ref, wpg_ref, gfin_ref = refs[5:5 + N_POST_W]
    y_ref, fbuf_ref, fext, act_scr = refs[5 + N_POST_W:]
    t = pl.program_id(1)
    _, HF = _halo(FFN_CONV_K, rs)
    _halo_begin(fext, fbuf0_ref, FFN_CONV_K, rs, R, t)

    mix = jnp.concatenate([conf_ref[...].astype(BF16), o_ref[...].astype(BF16)], axis=1)
    x1 = x_ref[...] + jnp.dot(mix, wout_ref[...], preferred_element_type=F32)
    hn = _rms(x1, gffn_ref[...]).astype(BF16)
    for j in range(2 * N_FF_CHUNKS):
        c0 = j * FF_COLS
        fext[HF:HF + R, c0:c0 + FF_COLS] = jnp.dot(hn, wup_ref[:, c0:c0 + FF_COLS],
                                                   preferred_element_type=F32)

    RC = 64
    for j in range(N_FF_CHUNKS):
        for r0 in range(0, R, RC):
            halves = []
            for c0 in (j * FF_COLS, D_FF + j * FF_COLS):
                cols = slice(c0, c0 + FF_COLS)
                halves.append(_causal_conv(fext, fw_ref, cols, FFN_CONV_K, rs, r0, RC) + fb_ref[:, cols])
            act_scr[r0:r0 + RC, j * FF_COLS:(j + 1) * FF_COLS] = (_silu(halves[0]) * halves[1]).astype(BF16)

    RB = 128
    for r0 in range(0, R, RB):
        rows = slice(r0, r0 + RB)
        x2 = x1[rows] + jnp.dot(act_scr[rows, :], wdown_ref[...], preferred_element_type=F32)
        ple = jnp.dot(p_ref[rows, :].astype(BF16), wple_ref[...], preferred_element_type=F32)
        x3 = x2 + ple * _sigmoid(jnp.dot(x2.astype(BF16), wpg_ref[...], preferred_element_type=F32))
        y_ref[rows, :] = _rms(x3, gfin_ref[...])

    _halo_end(fext, fbuf_ref, FFN_CONV_K, rs, R, t, NT)


def _postmix(x, conf, o, p, fbuf0, post_w, *, R, rs):
    G, L, _ = x.shape
    NT = L // R
    halo, HF = _halo(FFN_CONV_K, rs)

    def tile(c):
        return pl.BlockSpec((None, R, c), lambda gi, ti: (gi, ti, 0))

    def full(a):
        return pl.BlockSpec(a.shape, lambda gi, ti: (0,) * a.ndim)

    buf_spec = pl.BlockSpec((None, halo, 2 * D_FF), lambda gi, ti: (gi, 0, 0))
    return pl.pallas_call(
        functools.partial(_postmix_kernel, R=R, rs=rs, NT=NT),
        grid=(G, NT),
        in_specs=[tile(D_MODEL), tile(CONF_WIDTH), tile(DN_WIDTH), tile(D_PLE), buf_spec]
        + [full(a) for a in post_w],
        out_specs=[tile(D_MODEL), buf_spec],
        out_shape=[jax.ShapeDtypeStruct((G, L, D_MODEL), F32),
                   jax.ShapeDtypeStruct((G, halo, 2 * D_FF), F32)],
        scratch_shapes=[pltpu.VMEM((HF + R, 2 * D_FF), F32), pltpu.VMEM((R, D_FF), BF16)],
        compiler_params=pltpu.CompilerParams(dimension_semantics=("arbitrary", "arbitrary"),
                                             vmem_limit_bytes=VMEM_LIMIT),
        name="postmix",
    )(x, conf, o, p, fbuf0, *post_w)


def _sublane_replicated(w):
    return jnp.broadcast_to(w[:, None, :], (w.shape[0], SUBLANES, w.shape[1]))


def _interleave(a, rs):
    n, t, c = a.shape
    return a.reshape(n // rs, rs, t, c).transpose(0, 2, 1, 3).reshape(n // rs, t * rs, c)


def _deinterleave(a, rs, t):
    g, _, c = a.shape
    return a.reshape(g, t, rs, c).transpose(0, 2, 1, 3).reshape(g * rs, t, c)


def _trunk(x, p, conf_buf, dn_buf, s0, ffn_buf, pre_w, ng, post_w, *, interleaved):
    seq = x.shape[1]
    if interleaved:
        rs = SAMPLE_RS
        x, p, conf_buf, dn_buf, ffn_buf = (_interleave(a, rs) for a in (x, p, conf_buf, dn_buf, ffn_buf))
        pre_rows = post_rows = seq * rs
    else:
        rs = 1
        pre_rows, post_rows = PREMIX_ROWS, POSTMIX_ROWS
    conf, qkv, z, bg, conf_new, dn_new = _premix(x, conf_buf, dn_buf, pre_w, R=pre_rows, rs=rs)
    if interleaved:
        o, s_new = _delta_sample(qkv, z, bg, ng, s0, T=seq, rs=rs)
    else:
        o, s_new = _delta_prompt(qkv, z, bg, ng, s0, SB=DELTA_SEQS, R=DELTA_ROWS)
    y, ffn_new = _postmix(x, conf, o, p, ffn_buf, post_w, R=post_rows, rs=rs)
    if interleaved:
        y = _deinterleave(y, rs, seq)
        conf_new = _deinterleave(conf_new, rs, CONF_K - 1)
        dn_new = _deinterleave(dn_new, rs, DN_CONV_K - 1)
        ffn_new = _deinterleave(ffn_new, rs, FFN_CONV_K - 1)
    return y, conf_new[None], dn_new[None], s_new[None], ffn_new[None]


def kernel(x_prompt, x_sample, p_prompt, p_sample, state_conf_buf, state_dn_conv_buf, state_dn_S, state_ffn_buf,
           norm_mix_g, w_in, conf_dw_w, conf_dw_b, conf_ln_g, conf_ln_b, dn_conv_w, dn_a_log, dn_dt_bias,
           dn_norm_g, w_out, norm_ffn_g, w_up, ffn_conv_w, ffn_conv_b, w_down, w_ple, w_ple_gate, norm_final_g):
    assert w_in.shape[0] == 1, "single trunk layer"
    gate_pad = ((0, 0), (DN_HEADS, LANES - 2 * DN_HEADS))
    pre_w = (norm_mix_g[0][None], w_in[0].astype(BF16),
             jnp.pad(w_in[0][:, N_MAIN:], ((0, 0), (0, LANES - 2 * DN_HEADS))).astype(BF16),
             _sublane_replicated(conf_dw_w[0]), conf_dw_b[0][None], conf_ln_g[0][None], conf_ln_b[0][None],
             _sublane_replicated(dn_conv_w[0]), jnp.pad(dn_a_log[0][None], gate_pad),
             jnp.pad(dn_dt_bias[0][None], gate_pad))
    ng = dn_norm_g[0][None]
    post_w = (w_out[0].astype(BF16), norm_ffn_g[0][None], w_up[0].astype(BF16),
              _sublane_replicated(ffn_conv_w[0]), ffn_conv_b[0][None], w_down[0].astype(BF16),
              w_ple[0].astype(BF16), w_ple_gate[0].astype(BF16), norm_final_g[None])
    assert len(pre_w) == N_PRE_W and len(post_w) == N_POST_W

    bp, dt = x_prompt.shape[0], x_prompt.dtype
    z_conf = jnp.zeros((bp, CONF_K - 1, CONF_WIDTH), dt)
    z_dn = jnp.zeros((bp, DN_CONV_K - 1, 3 * DN_WIDTH), dt)
    z_s = jnp.zeros((bp, DN_HEADS, DN_HEAD_DIM, DN_HEAD_DIM), dt)
    z_ffn = jnp.zeros((bp, FFN_CONV_K - 1, 2 * D_FF), dt)
    yp, pc, pd, ps, pf = _trunk(x_prompt, p_prompt[0], z_conf, z_dn, z_s, z_ffn, pre_w, ng, post_w,
                                interleaved=False)
    ys, sc, sd, ss, sf = _trunk(x_sample, p_sample[0], state_conf_buf[0], state_dn_conv_buf[0], state_dn_S[0],
                                state_ffn_buf[0], pre_w, ng, post_w, interleaved=True)
    return (yp, ys, pc, pd, ps, pf, sc, sd, ss, sf)
```

```python
import functools

import jax
import jax.numpy as jnp
from jax import lax
from jax.experimental import pallas as pl
from jax.experimental.pallas import tpu as pltpu

F32 = jnp.float32
BF16 = jnp.bfloat16

D_MODEL = 1024
D_PLE = 256
CONF_WIDTH = 512
CONF_K = 31
DN_HEADS = 4
DN_HEAD_DIM = 128
DN_WIDTH = DN_HEADS * DN_HEAD_DIM
DN_CONV_K = 4
D_FF = 2816
FFN_CONV_K = 3
CHUNK = 64
EPS = 1e-6
N_MAIN = 2 * CONF_WIDTH + 4 * DN_WIDTH

SUBLANES = 8
LANES = 128
FF_COLS = 256
N_FF_CHUNKS = D_FF // FF_COLS
SAMPLE_RS = 32
PREMIX_ROWS = 512
DELTA_ROWS = 128
DELTA_SEQS = 8
POSTMIX_ROWS = 512
VMEM_LIMIT = 56 * 1024 * 1024
N_PRE_W = 10
N_POST_W = 9


def _round_up(n, m):
    return (n + m - 1) // m * m


def _mm(a, b):
    return jnp.dot(a.astype(BF16), b.astype(BF16), preferred_element_type=F32)


def _mm_nt(a, b):
    return lax.dot_general(a.astype(BF16), b.astype(BF16), (((1,), (1,)), ((), ())),
                           preferred_element_type=F32)


def _bmm(a, b):
    return jnp.einsum("nij,njk->nik", a.astype(BF16), b.astype(BF16), preferred_element_type=F32)


def _bmm_nt(a, b):
    return jnp.einsum("nid,njd->nij", a.astype(BF16), b.astype(BF16), preferred_element_type=F32)


def _bmm_tn(a, b):
    return jnp.einsum("nck,ncv->nkv", a.astype(BF16), b.astype(BF16), preferred_element_type=F32)


def _split3(x):
    hi = x.astype(BF16)
    r1 = x - hi.astype(F32)
    mid = r1.astype(BF16)
    lo = (r1 - mid.astype(F32)).astype(BF16)
    return hi, mid, lo


def _mm_exact_lhs(m_bf16, x):
    hi, mid, lo = _split3(x)
    dot = functools.partial(jnp.dot, preferred_element_type=F32)
    return dot(m_bf16, hi) + dot(m_bf16, mid) + dot(m_bf16, lo)


def _sigmoid(x):
    return 1.0 / (1.0 + jnp.exp(-x))


def _silu(x):
    return x * _sigmoid(x)


def _softplus(x):
    return jnp.maximum(x, 0.0) + jnp.log1p(jnp.exp(-jnp.abs(x)))


def _rms(x, g):
    return x * lax.rsqrt(jnp.mean(x * x, axis=-1, keepdims=True) + EPS) * g


def _halo(taps, rs):
    rows = (taps - 1) * rs
    return rows, _round_up(rows, SUBLANES)


def _halo_begin(ext, buf0_ref, taps, rs, R, t):
    rows, padded = _halo(taps, rs)

    @pl.when(t == 0)
    def _():
        ext[padded - rows:padded, :] = buf0_ref[...]

    @pl.when(t > 0)
    def _():
        ext[0:padded, :] = ext[R:R + padded, :]


def _halo_end(ext, buf_ref, taps, rs, R, t, NT):
    rows, padded = _halo(taps, rs)

    @pl.when(t == NT - 1)
    def _():
        buf_ref[...] = ext[padded + R - rows:padded + R, :]


def _causal_conv(ext_ref, w_ref, cols, taps, rs, r0, n_rows):
    _, padded = _halo(taps, rs)
    groups = {}
    for k in range(taps):
        off = padded - (taps - 1 - k) * rs
        groups.setdefault(off % SUBLANES, []).append((k, off - off % SUBLANES))
    total = None
    for mis, members in sorted(groups.items()):
        n = n_rows + (SUBLANES if mis else 0)
        part = None
        for k, base in members:
            w_rows = jnp.concatenate([w_ref[k, :, cols]] * (n // SUBLANES), axis=0)
            term = w_rows * ext_ref[base + r0:base + r0 + n, cols]
            part = term if part is None else part + term
        if mis:
            part = part[mis:mis + n_rows]
        total = part if total is None else total + part
    return total


def _premix_kernel(*refs, R, rs, NT):
    x_ref = refs[0]
    g_ref, win_ref, wba_ref, cw_ref, cb_ref, lng_ref, lnb_ref, dw_ref, alog_ref, dtb_ref = refs[1:1 + N_PRE_W]
    cbuf0_ref, dbuf0_ref, conf_ref, qkv_ref, z_ref, bg_ref, cbuf_ref, dbuf_ref, cext, dext = refs[1 + N_PRE_W:]
    t = pl.program_id(1)
    _, HC = _halo(CONF_K, rs)
    _, HD = _halo(DN_CONV_K, rs)
    q0 = 2 * CONF_WIDTH
    z0 = q0 + 3 * DN_WIDTH
    _halo_begin(cext, cbuf0_ref, CONF_K, rs, R, t)
    _halo_begin(dext, dbuf0_ref, DN_CONV_K, rs, R, t)

    h = _rms(x_ref[...], g_ref[...]).astype(BF16)
    RG = 128
    for r0 in range(0, R, RG):
        vg = jnp.dot(h[r0:r0 + RG], win_ref[:, 0:q0], preferred_element_type=F32)
        cext[HC + r0:HC + r0 + RG, :] = vg[:, :CONF_WIDTH] * _sigmoid(vg[:, CONF_WIDTH:])

    def zero_row_after(piece):
        bits = pltpu.bitcast(piece[0:SUBLANES, 0:LANES], jnp.uint32)
        zero = pltpu.bitcast(lax.shift_right_logical(lax.shift_right_logical(bits, jnp.uint32(16)), jnp.uint32(16)), F32)
        return jnp.concatenate([zero[0:1, :]] * (CONF_WIDTH // LANES), axis=1)

    ordering_zeros = []
    for c0 in range(q0, N_MAIN, FF_COLS):
        piece = jnp.dot(h, win_ref[:, c0:c0 + FF_COLS], preferred_element_type=F32)
        if c0 < z0:
            dext[HD:HD + R, c0 - q0:c0 - q0 + FF_COLS] = piece
        else:
            z_ref[:, c0 - z0:c0 - z0 + FF_COLS] = piece
        ordering_zeros.append(zero_row_after(piece))

    ba = jnp.dot(h, wba_ref[...], preferred_element_type=F32)
    lane = lax.broadcasted_iota(jnp.int32, ba.shape, 1)
    gate = -jnp.exp(alog_ref[...]) * _softplus(ba + dtb_ref[...])
    bg_ref[...] = jnp.where(lane < DN_HEADS, _sigmoid(ba), gate)

    RC = 64
    for r0 in range(0, R, RC):
        bias = cb_ref[...] + ordering_zeros[min(r0 // RC * len(ordering_zeros) // (R // RC), len(ordering_zeros) - 1)]
        acc = _causal_conv(cext, cw_ref, slice(None), CONF_K, rs, r0, RC) + bias
        mu = jnp.mean(acc, axis=-1, keepdims=True)
        cen = acc - mu
        var = jnp.mean(cen * cen, axis=-1, keepdims=True)
        n = cen * lax.rsqrt(var + EPS) * lng_ref[...] + lnb_ref[...]
        conf_ref[r0:r0 + RC, :] = _silu(n)

    RD = 32
    for r0 in range(0, R, RD):
        y = _silu(_causal_conv(dext, dw_ref, slice(None), DN_CONV_K, rs, r0, RD))
        for j in range(3 * DN_HEADS):
            seg = y[:, j * DN_HEAD_DIM:(j + 1) * DN_HEAD_DIM]
            if j < 2 * DN_HEADS:
                seg = seg * lax.rsqrt(jnp.sum(seg * seg, axis=-1, keepdims=True) + EPS)
            qkv_ref[r0:r0 + RD, j * DN_HEAD_DIM:(j + 1) * DN_HEAD_DIM] = seg

    _halo_end(cext, cbuf_ref, CONF_K, rs, R, t, NT)
    _halo_end(dext, dbuf_ref, DN_CONV_K, rs, R, t, NT)


def _premix(x, cbuf0, dbuf0, pre_w, *, R, rs):
    G, L, _ = x.shape
    NT = L // R
    c_halo, HC = _halo(CONF_K, rs)
    d_halo, HD = _halo(DN_CONV_K, rs)

    def tile(c):
        return pl.BlockSpec((None, R, c), lambda gi, ti: (gi, ti, 0))

    def full(a):
        return pl.BlockSpec(a.shape, lambda gi, ti: (0,) * a.ndim)

    def per_group(rows, c):
        return pl.BlockSpec((None, rows, c), lambda gi, ti: (gi, 0, 0))

    return pl.pallas_call(
        functools.partial(_premix_kernel, R=R, rs=rs, NT=NT),
        grid=(G, NT),
        in_specs=[tile(D_MODEL)] + [full(a) for a in pre_w]
        + [per_group(c_halo, CONF_WIDTH), per_group(d_halo, 3 * DN_WIDTH)],
        out_specs=[tile(CONF_WIDTH), tile(3 * DN_WIDTH), tile(DN_WIDTH), tile(LANES),
                   per_group(c_halo, CONF_WIDTH), per_group(d_halo, 3 * DN_WIDTH)],
        out_shape=[jax.ShapeDtypeStruct((G, L, CONF_WIDTH), F32),
                   jax.ShapeDtypeStruct((G, L, 3 * DN_WIDTH), F32),
                   jax.ShapeDtypeStruct((G, L, DN_WIDTH), F32),
                   jax.ShapeDtypeStruct((G, L, LANES), F32),
                   jax.ShapeDtypeStruct((G, c_halo, CONF_WIDTH), F32),
                   jax.ShapeDtypeStruct((G, d_halo, 3 * DN_WIDTH), F32)],
        scratch_shapes=[pltpu.VMEM((HC + R, CONF_WIDTH), F32), pltpu.VMEM((HD + R, 3 * DN_WIDTH), F32)],
        compiler_params=pltpu.CompilerParams(dimension_semantics=("arbitrary", "arbitrary"),
                                             vmem_limit_bytes=VMEM_LIMIT),
        name="premix",
    )(x, *pre_w, cbuf0, dbuf0)


def _neumann(a, eye, squarings, mm):
    x = eye - a
    p = a
    for _ in range(squarings):
        p = mm(p, p)
        x = x + mm(x, p)
    return x


def _head_norm_gate(o, z, ng):
    return o * lax.rsqrt(jnp.mean(o * o, axis=-1, keepdims=True) + EPS) * ng * _silu(z)


def _delta_prompt_kernel(qkv_ref, z_ref, bg_ref, ng_ref, s0_ref, o_ref, snew_ref, s_scr, *, SB, R, NT):
    t = pl.program_id(1)
    state_shape = (SB * DN_HEADS, DN_HEAD_DIM, DN_HEAD_DIM)

    @pl.when(t == 0)
    def _():
        s_scr[...] = s0_ref[...].reshape(state_shape)

    ri = lax.broadcasted_iota(jnp.int32, (R, R), 0)
    ci = lax.broadcasted_iota(jnp.int32, (R, R), 1)
    same_chunk = lax.shift_right_logical(ri, 6) == lax.shift_right_logical(ci, 6)
    csum = jnp.where(same_chunk & (ci <= ri), 1.0, 0.0).astype(BF16)
    bg = [bg_ref[b] for b in range(SB)]
    gc_all = [_mm_exact_lhs(csum, bg[b]) for b in range(SB)]
    gc_t = [g.T for g in gc_all]

    ri64 = lax.broadcasted_iota(jnp.int32, (CHUNK, CHUNK), 0)
    ci64 = lax.broadcasted_iota(jnp.int32, (CHUNK, CHUNK), 1)
    causal = ri64 >= ci64
    strict = ri64 > ci64
    eye = (ri64 == ci64).astype(F32)
    same_block = lax.shift_right_logical(ri64, 4) == lax.shift_right_logical(ci64, 4)
    scale = DN_HEAD_DIM ** -0.5
    NC = R // CHUNK
    per_chunk = SB * DN_HEADS
    blocks = [(b, c * CHUNK, h) for c in range(NC) for b in range(SB) for h in range(DN_HEADS)]

    def heads(col0):
        return jnp.stack([qkv_ref[b, r0:r0 + CHUNK, col0 + h * DN_HEAD_DIM:col0 + (h + 1) * DN_HEAD_DIM]
                          for b, r0, h in blocks])

    q = heads(0) * scale
    k = heads(DN_WIDTH)
    v = heads(2 * DN_WIDTH)
    beta = jnp.stack([bg[b][r0:r0 + CHUNK, h:h + 1] for b, r0, h in blocks])
    gcc = jnp.stack([gc_all[b][r0:r0 + CHUNK, DN_HEADS + h:DN_HEADS + h + 1] for b, r0, h in blocks])
    gcr = jnp.stack([gc_t[b][DN_HEADS + h:DN_HEADS + h + 1, r0:r0 + CHUNK] for b, r0, h in blocks])
    glast = gcc[:, CHUNK - 1:CHUNK, :]
    decay = jnp.exp(jnp.where(causal, gcc - gcr, -jnp.inf))
    kb = k * beta
    a = jnp.where(strict, _bmm_nt(kb, k) * decay, 0.0)
    a_diag = jnp.where(same_block, a, 0.0)
    a_off = jnp.where(same_block, 0.0, a)
    d = _neumann(a_diag, eye, 3, _bmm)
    b = _bmm(d, a_off)
    tinv = _bmm(_neumann(b, eye, 1, _bmm), d)
    eg = jnp.exp(gcc)
    uw = _bmm(tinv, jnp.concatenate([v * beta, kb * eg], axis=2))
    u = uw[:, :, :DN_HEAD_DIM]
    w = uw[:, :, DN_HEAD_DIM:]
    qk = jnp.where(causal, _bmm_nt(q, k) * decay, 0.0)
    qd = q * eg
    kd = k * jnp.exp(glast - gcc)
    eglast = jnp.exp(glast)

    s = s_scr[...]
    for c in range(NC):
        sl = slice(c * per_chunk, (c + 1) * per_chunk)
        v_new = u[sl] - _bmm(w[sl], s)
        o = _bmm(qd[sl], s) + _bmm(qk[sl], v_new)
        s = s * eglast[sl] + _bmm_tn(kd[sl], v_new)
        for n, (b, r0, h) in enumerate(blocks[sl]):
            lo = h * DN_HEAD_DIM
            o_ref[b, r0:r0 + CHUNK, lo:lo + DN_HEAD_DIM] = _head_norm_gate(
                o[n], z_ref[b, r0:r0 + CHUNK, lo:lo + DN_HEAD_DIM], ng_ref[...])
    s_scr[...] = s

    @pl.when(t == NT - 1)
    def _():
        snew_ref[...] = s_scr[...].reshape(snew_ref.shape)


def _delta_prompt(qkv, z, bg, ng, s0, *, SB, R):
    B, L, _ = qkv.shape
    NT = L // R

    def tile(c):
        return pl.BlockSpec((SB, R, c), lambda bi, ti: (bi, ti, 0))

    s_spec = pl.BlockSpec((SB, DN_HEADS, DN_HEAD_DIM, DN_HEAD_DIM), lambda bi, ti: (bi, 0, 0, 0))
    return pl.pallas_call(
        functools.partial(_delta_prompt_kernel, SB=SB, R=R, NT=NT),
        grid=(B // SB, NT),
        in_specs=[tile(3 * DN_WIDTH), tile(DN_WIDTH), tile(LANES),
                  pl.BlockSpec(ng.shape, lambda bi, ti: (0, 0)), s_spec],
        out_specs=[tile(DN_WIDTH), s_spec],
        out_shape=[jax.ShapeDtypeStruct((B, L, DN_WIDTH), F32), jax.ShapeDtypeStruct(s0.shape, F32)],
        scratch_shapes=[pltpu.VMEM((SB * DN_HEADS, DN_HEAD_DIM, DN_HEAD_DIM), F32)],
        compiler_params=pltpu.CompilerParams(dimension_semantics=("arbitrary", "arbitrary"),
                                             vmem_limit_bytes=VMEM_LIMIT),
        name="delta_prompt",
    )(qkv, z, bg, ng, s0)


def _delta_sample_kernel(qkv_ref, z_ref, bg_ref, ng_ref, s0_ref, o_ref, snew_ref,
                         w_scr, qd_scr, kd_scr, u_scr, vn_scr, os_scr, *, T, rs):
    R = T * rs
    shift = rs.bit_length() - 1
    bg = bg_ref[...]
    ri = lax.broadcasted_iota(jnp.int32, (R, R), 0)
    ci = lax.broadcasted_iota(jnp.int32, (R, R), 1)
    same_seq = (ri & (rs - 1)) == (ci & (rs - 1))
    ti = lax.shift_right_logical(ri, shift)
    tj = lax.shift_right_logical(ci, shift)
    causal = same_seq & (ti >= tj)
    strict = same_seq & (ti > tj)
    eye = (ri == ci).astype(F32)
    gc_all = _mm_exact_lhs(jnp.where(causal, 1.0, 0.0).astype(BF16), bg)
    gc_t = gc_all.T
    scale = DN_HEAD_DIM ** -0.5
    squarings = (T - 1).bit_length() - 1
    seq_rows = [pl.ds(s, T, stride=rs) for s in range(rs)]

    for h in range(DN_HEADS):
        lo = h * DN_HEAD_DIM
        q = qkv_ref[:, lo:lo + DN_HEAD_DIM] * scale
        k = qkv_ref[:, DN_WIDTH + lo:DN_WIDTH + lo + DN_HEAD_DIM]
        v = qkv_ref[:, 2 * DN_WIDTH + lo:2 * DN_WIDTH + lo + DN_HEAD_DIM]
        beta = bg[:, h:h + 1]
        gcc = gc_all[:, DN_HEADS + h:DN_HEADS + h + 1]
        gcr = gc_t[DN_HEADS + h:DN_HEADS + h + 1, :]
        glast = gc_all[R - rs:R, DN_HEADS + h:DN_HEADS + h + 1]
        glast_rows = jnp.concatenate([glast] * T, axis=0)
        decay = jnp.exp(jnp.where(causal, gcc - gcr, -jnp.inf))
        kb = k * beta
        a = jnp.where(strict, _mm_nt(kb, k) * decay, 0.0)
        tinv = _neumann(a, eye, squarings, _mm)
        eg = jnp.exp(gcc)
        uw = _mm(tinv, jnp.concatenate([v * beta, kb * eg], axis=1))
        u_scr[...] = uw[:, :DN_HEAD_DIM]
        w_scr[...] = uw[:, DN_HEAD_DIM:]
        qd_scr[...] = q * eg
        kd_scr[...] = k * jnp.exp(glast_rows - gcc)
        qk = jnp.where(causal, _mm_nt(q, k) * decay, 0.0)
        eglast = jnp.exp(glast)
        wq = jnp.stack([jnp.concatenate([w_scr[rows, :], qd_scr[rows, :]], axis=0) for rows in seq_rows])
        us = jnp.stack([u_scr[rows, :] for rows in seq_rows])
        kds = jnp.stack([kd_scr[rows, :] for rows in seq_rows])
        egl = jnp.stack([eglast[s:s + 1, :] for s in range(rs)])
        state = s0_ref[:, h]
        ws = _bmm(wq, state)
        v_new = us - ws[:, :T]
        snew_ref[:, h] = state * egl + _bmm_tn(kds, v_new)
        for s, rows in enumerate(seq_rows):
            vn_scr[rows, :] = v_new[s]
            os_scr[rows, :] = ws[s, T:]
        o = os_scr[...] + _mm(qk, vn_scr[...])
        o_ref[:, lo:lo + DN_HEAD_DIM] = _head_norm_gate(o, z_ref[:, lo:lo + DN_HEAD_DIM], ng_ref[...])


def _delta_sample(qkv, z, bg, ng, s0, *, T, rs):
    G, R, _ = qkv.shape

    def tile(c):
        return pl.BlockSpec((None, R, c), lambda gi: (gi, 0, 0))

    s_spec = pl.BlockSpec((rs, DN_HEADS, DN_HEAD_DIM, DN_HEAD_DIM), lambda gi: (gi, 0, 0, 0))
    return pl.pallas_call(
        functools.partial(_delta_sample_kernel, T=T, rs=rs),
        grid=(G,),
        in_specs=[tile(3 * DN_WIDTH), tile(DN_WIDTH), tile(LANES),
                  pl.BlockSpec(ng.shape, lambda gi: (0, 0)), s_spec],
        out_specs=[tile(DN_WIDTH), s_spec],
        out_shape=[jax.ShapeDtypeStruct((G, R, DN_WIDTH), F32), jax.ShapeDtypeStruct(s0.shape, F32)],
        scratch_shapes=[pltpu.VMEM((R, DN_HEAD_DIM), F32) for _ in range(6)],
        compiler_params=pltpu.CompilerParams(dimension_semantics=("arbitrary",),
                                             vmem_limit_bytes=VMEM_LIMIT),
        name="delta_sample",
    )(qkv, z, bg, ng, s0)


def _postmix_kernel(*refs, R, rs, NT):
    x_ref, conf_ref, o_ref, p_ref, fbuf0_ref = refs[:5]
    wout_ref, gffn_ref, wup_ref, fw_ref, fb_ref, wdown_ref, wple_ref, wpg_ref, gfin_ref = refs[5:5 + N_POST_W]
    y_ref, fbuf_ref, fext, act_scr = refs[5 + N_POST_W:]
    t = pl.program_id(1)
    _, HF = _halo(FFN_CONV_K, rs)
    _halo_begin(fext, fbuf0_ref, FFN_CONV_K, rs, R, t)

    mix = jnp.concatenate([conf_ref[...].astype(BF16), o_ref[...].astype(BF16)], axis=1)
    x1 = x_ref[...] + jnp.dot(mix, wout_ref[...], preferred_element_type=F32)
    hn = _rms(x1, gffn_ref[...]).astype(BF16)
    for j in range(2 * N_FF_CHUNKS):
        c0 = j * FF_COLS
        fext[HF:HF + R, c0:c0 + FF_COLS] = jnp.dot(hn, wup_ref[:, c0:c0 + FF_COLS],
                                                   preferred_element_type=F32)

    RC = 64
    for j in range(N_FF_CHUNKS):
        for r0 in range(0, R, RC):
            halves = []
            for c0 in (j * FF_COLS, D_FF + j * FF_COLS):
                cols = slice(c0, c0 + FF_COLS)
                halves.append(_causal_conv(fext, fw_ref, cols, FFN_CONV_K, rs, r0, RC) + fb_ref[:, cols])
            act_scr[r0:r0 + RC, j * FF_COLS:(j + 1) * FF_COLS] = (_silu(halves[0]) * halves[1]).astype(BF16)

    RB = 128
    for r0 in range(0, R, RB):
        rows = slice(r0, r0 + RB)
        x2 = x1[rows] + jnp.dot(act_scr[rows, :], wdown_ref[...], preferred_element_type=F32)
        ple = jnp.dot(p_ref[rows, :].astype(BF16), wple_ref[...], preferred_element_type=F32)
        x3 = x2 + ple * _sigmoid(jnp.dot(x2.astype(BF16), wpg_ref[...], preferred_element_type=F32))
        y_ref[rows, :] = _rms(x3, gfin_ref[...])

    _halo_end(fext, fbuf_ref, FFN_CONV_K, rs, R, t, NT)


def _postmix(x, conf, o, p, fbuf0, post_w, *, R, rs):
    G, L, _ = x.shape
    NT = L // R
    halo, HF = _halo(FFN_CONV_K, rs)

    def tile(c):
        return pl.BlockSpec((None, R, c), lambda gi, ti: (gi, ti, 0))

    def full(a):
        return pl.BlockSpec(a.shape, lambda gi, ti: (0,) * a.ndim)

    buf_spec = pl.BlockSpec((None, halo, 2 * D_FF), lambda gi, ti: (gi, 0, 0))
    return pl.pallas_call(
        functools.partial(_postmix_kernel, R=R, rs=rs, NT=NT),
        grid=(G, NT),
        in_specs=[tile(D_MODEL), tile(CONF_WIDTH), tile(DN_WIDTH), tile(D_PLE), buf_spec]
        + [full(a) for a in post_w],
        out_specs=[tile(D_MODEL), buf_spec],
        out_shape=[jax.ShapeDtypeStruct((G, L, D_MODEL), F32),
                   jax.ShapeDtypeStruct((G, halo, 2 * D_FF), F32)],
        scratch_shapes=[pltpu.VMEM((HF + R, 2 * D_FF), F32), pltpu.VMEM((R, D_FF), BF16)],
        compiler_params=pltpu.CompilerParams(dimension_semantics=("arbitrary", "arbitrary"),
                                             vmem_limit_bytes=VMEM_LIMIT),
        name="postmix",
    )(x, conf, o, p, fbuf0, *post_w)


CAST_STEPS = 8


def _cast_kernel(*refs):
    n = len(refs) // 2
    for src, dst in zip(refs[:n], refs[n:]):
        dst[...] = src[...].astype(BF16)


def _to_bf16(*weights):
    def rows(a):
        return pl.BlockSpec((a.shape[0] // CAST_STEPS, a.shape[1]), lambda i: (i, 0))

    return pl.pallas_call(
        _cast_kernel,
        grid=(CAST_STEPS,),
        in_specs=[rows(a) for a in weights],
        out_specs=[rows(a) for a in weights],
        out_shape=[jax.ShapeDtypeStruct(a.shape, BF16) for a in weights],
        compiler_params=pltpu.CompilerParams(dimension_semantics=("arbitrary",), vmem_limit_bytes=VMEM_LIMIT),
        name="weights_bf16",
    )(*weights)


def _sublane_replicated(w):
    return jnp.broadcast_to(w[:, None, :], (w.shape[0], SUBLANES, w.shape[1]))


def _interleave(a, rs):
    n, t, c = a.shape
    return a.reshape(n // rs, rs, t, c).transpose(0, 2, 1, 3).reshape(n // rs, t * rs, c)


def _deinterleave(a, rs, t):
    g, _, c = a.shape
    return a.reshape(g, t, rs, c).transpose(0, 2, 1, 3).reshape(g * rs, t, c)


def _trunk(x, p, conf_buf, dn_buf, s0, ffn_buf, pre_w, ng, post_w, *, interleaved):
    seq = x.shape[1]
    if interleaved:
        rs = SAMPLE_RS
        x, p, conf_buf, dn_buf, ffn_buf = (_interleave(a, rs) for a in (x, p, conf_buf, dn_buf, ffn_buf))
        pre_rows = post_rows = seq * rs
    else:
        rs = 1
        pre_rows, post_rows = PREMIX_ROWS, POSTMIX_ROWS
    conf, qkv, z, bg, conf_new, dn_new = _premix(x, conf_buf, dn_buf, pre_w, R=pre_rows, rs=rs)
    if interleaved:
        o, s_new = _delta_sample(qkv, z, bg, ng, s0, T=seq, rs=rs)
    else:
        o, s_new = _delta_prompt(qkv, z, bg, ng, s0, SB=DELTA_SEQS, R=DELTA_ROWS)
    y, ffn_new = _postmix(x, conf, o, p, ffn_buf, post_w, R=post_rows, rs=rs)
    if interleaved:
        y = _deinterleave(y, rs, seq)
        conf_new = _deinterleave(conf_new, rs, CONF_K - 1)
        dn_new = _deinterleave(dn_new, rs, DN_CONV_K - 1)
        ffn_new = _deinterleave(ffn_new, rs, FFN_CONV_K - 1)
    return y, conf_new[None], dn_new[None], s_new[None], ffn_new[None]


def kernel(x_prompt, x_sample, p_prompt, p_sample, state_conf_buf, state_dn_conv_buf, state_dn_S, state_ffn_buf,
           norm_mix_g, w_in, conf_dw_w, conf_dw_b, conf_ln_g, conf_ln_b, dn_conv_w, dn_a_log, dn_dt_bias,
           dn_norm_g, w_out, norm_ffn_g, w_up, ffn_conv_w, ffn_conv_b, w_down, w_ple, w_ple_gate, norm_final_g):
    assert w_in.shape[0] == 1, "single trunk layer"
    gate_pad = ((0, 0), (DN_HEADS, LANES - 2 * DN_HEADS))
    w_in_b, w_out_b, w_up_b, w_down_b, w_ple_b, w_pg_b = _to_bf16(
        w_in[0], w_out[0], w_up[0], w_down[0], w_ple[0], w_ple_gate[0])
    pre_w = (norm_mix_g[0][None], w_in_b,
             jnp.pad(w_in[0][:, N_MAIN:], ((0, 0), (0, LANES - 2 * DN_HEADS))).astype(BF16),
             _sublane_replicated(conf_dw_w[0]), conf_dw_b[0][None], conf_ln_g[0][None], conf_ln_b[0][None],
             _sublane_replicated(dn_conv_w[0]), jnp.pad(dn_a_log[0][None], gate_pad),
             jnp.pad(dn_dt_bias[0][None], gate_pad))
    ng = dn_norm_g[0][None]
    post_w = (w_out_b, norm_ffn_g[0][None], w_up_b,
              _sublane_replicated(ffn_conv_w[0]), ffn_conv_b[0][None], w_down_b,
              w_ple_b, w_pg_b, norm_final_g[None])
    assert len(pre_w) == N_PRE_W and len(post_w) == N_POST_W

    bp, dt = x_prompt.shape[0], x_prompt.dtype
    z_conf = jnp.zeros((bp, CONF_K - 1, CONF_WIDTH), dt)
    z_dn = jnp.zeros((bp, DN_CONV_K - 1, 3 * DN_WIDTH), dt)
    z_s = jnp.zeros((bp, DN_HEADS, DN_HEAD_DIM, DN_HEAD_DIM), dt)
    z_ffn = jnp.zeros((bp, FFN_CONV_K - 1, 2 * D_FF), dt)
    yp, pc, pd, ps, pf = _trunk(x_prompt, p_prompt[0], z_conf, z_dn, z_s, z_ffn, pre_w, ng, post_w,
                                interleaved=False)
    ys, sc, sd, ss, sf = _trunk(x_sample, p_sample[0], state_conf_buf[0], state_dn_conv_buf[0], state_dn_S[0],
                                state_ffn_buf[0], pre_w, ng, post_w, interleaved=True)
    return (yp, ys, pc, pd, ps, pf, sc, sd, ss, sf)
```

```python
import functools

import jax
import jax.numpy as jnp
from jax import lax
from jax.experimental import pallas as pl
from jax.experimental.pallas import tpu as pltpu

F32 = jnp.float32
BF16 = jnp.bfloat16

D_MODEL = 1024
D_PLE = 256
CONF_WIDTH = 512
CONF_K = 31
DN_HEADS = 4
DN_HEAD_DIM = 128
DN_WIDTH = DN_HEADS * DN_HEAD_DIM
DN_CONV_K = 4
D_FF = 2816
FFN_CONV_K = 3
CHUNK = 64
EPS = 1e-6
N_MAIN = 2 * CONF_WIDTH + 4 * DN_WIDTH

SUBLANES = 8
LANES = 128
FF_COLS = 256
N_FF_CHUNKS = D_FF // FF_COLS
SAMPLE_RS = 32
PREMIX_ROWS = 512
DELTA_ROWS = 128
DELTA_SEQS = 8
POSTMIX_ROWS = 512
VMEM_LIMIT = 56 * 1024 * 1024
N_PRE_W = 10
N_POST_W = 9


def _round_up(n, m):
    return (n + m - 1) // m * m


def _mm(a, b):
    return jnp.dot(a.astype(BF16), b.astype(BF16), preferred_element_type=F32)


def _mm_nt(a, b):
    return lax.dot_general(a.astype(BF16), b.astype(BF16), (((1,), (1,)), ((), ())),
                           preferred_element_type=F32)


def _bmm(a, b):
    return jnp.einsum("nij,njk->nik", a.astype(BF16), b.astype(BF16), preferred_element_type=F32)


def _bmm_nt(a, b):
    return jnp.einsum("nid,njd->nij", a.astype(BF16), b.astype(BF16), preferred_element_type=F32)


def _bmm_tn(a, b):
    return jnp.einsum("nck,ncv->nkv", a.astype(BF16), b.astype(BF16), preferred_element_type=F32)


def _split3(x):
    hi = x.astype(BF16)
    r1 = x - hi.astype(F32)
    mid = r1.astype(BF16)
    lo = (r1 - mid.astype(F32)).astype(BF16)
    return hi, mid, lo


def _mm_exact_lhs(m_bf16, x):
    hi, mid, lo = _split3(x)
    dot = functools.partial(jnp.dot, preferred_element_type=F32)
    return dot(m_bf16, hi) + dot(m_bf16, mid) + dot(m_bf16, lo)


def _sigmoid(x):
    return 1.0 / (1.0 + jnp.exp(-x))


def _silu(x):
    return x * _sigmoid(x)


def _softplus(x):
    return jnp.maximum(x, 0.0) + jnp.log1p(jnp.exp(-jnp.abs(x)))


def _rms(x, g):
    return x * lax.rsqrt(jnp.mean(x * x, axis=-1, keepdims=True) + EPS) * g


def _halo(taps, rs):
    rows = (taps - 1) * rs
    return rows, _round_up(rows, SUBLANES)


def _halo_begin(ext, buf0_ref, taps, rs, R, t):
    rows, padded = _halo(taps, rs)

    @pl.when(t == 0)
    def _():
        ext[padded - rows:padded, :] = buf0_ref[...]

    @pl.when(t > 0)
    def _():
        ext[0:padded, :] = ext[R:R + padded, :]


def _halo_end(ext, buf_ref, taps, rs, R, t, NT):
    rows, padded = _halo(taps, rs)

    @pl.when(t == NT - 1)
    def _():
        buf_ref[...] = ext[padded + R - rows:padded + R, :]


def _causal_conv(ext_ref, w_ref, cols, taps, rs, r0, n_rows):
    _, padded = _halo(taps, rs)
    groups = {}
    for k in range(taps):
        off = padded - (taps - 1 - k) * rs
        groups.setdefault(off % SUBLANES, []).append((k, off - off % SUBLANES))
    total = None
    for mis, members in sorted(groups.items()):
        n = n_rows + (SUBLANES if mis else 0)
        part = None
        for k, base in members:
            w_rows = jnp.concatenate([w_ref[k, :, cols]] * (n // SUBLANES), axis=0)
            term = w_rows * ext_ref[base + r0:base + r0 + n, cols]
            part = term if part is None else part + term
        if mis:
            part = part[mis:mis + n_rows]
        total = part if total is None else total + part
    return total


def _premix_kernel(*refs, R, rs, NT):
    x_ref = refs[0]
    g_ref, win_ref, wba_ref, cw_ref, cb_ref, lng_ref, lnb_ref, dw_ref, alog_ref, dtb_ref = refs[1:1 + N_PRE_W]
    cbuf0_ref, dbuf0_ref, conf_ref, qkv_ref, z_ref, bg_ref, cbuf_ref, dbuf_ref, cext, dext = refs[1 + N_PRE_W:]
    t = pl.program_id(1)
    _, HC = _halo(CONF_K, rs)
    _, HD = _halo(DN_CONV_K, rs)
    q0 = 2 * CONF_WIDTH
    z0 = q0 + 3 * DN_WIDTH
    _halo_begin(cext, cbuf0_ref, CONF_K, rs, R, t)
    _halo_begin(dext, dbuf0_ref, DN_CONV_K, rs, R, t)

    h = _rms(x_ref[...], g_ref[...]).astype(BF16)
    RG = 128
    for r0 in range(0, R, RG):
        vg = jnp.dot(h[r0:r0 + RG], win_ref[:, 0:q0], preferred_element_type=F32)
        cext[HC + r0:HC + r0 + RG, :] = vg[:, :CONF_WIDTH] * _sigmoid(vg[:, CONF_WIDTH:])

    def zero_row_after(piece):
        bits = pltpu.bitcast(piece[0:SUBLANES, 0:LANES], jnp.uint32)
        zero = pltpu.bitcast(lax.shift_right_logical(lax.shift_right_logical(bits, jnp.uint32(16)), jnp.uint32(16)), F32)
        return jnp.concatenate([zero[0:1, :]] * (CONF_WIDTH // LANES), axis=1)

    ordering_zeros = []
    for c0 in range(q0, N_MAIN, FF_COLS):
        piece = jnp.dot(h, win_ref[:, c0:c0 + FF_COLS], preferred_element_type=F32)
        if c0 < z0:
            dext[HD:HD + R, c0 - q0:c0 - q0 + FF_COLS] = piece
        else:
            z_ref[:, c0 - z0:c0 - z0 + FF_COLS] = piece
        ordering_zeros.append(zero_row_after(piece))

    ba = jnp.dot(h, wba_ref[...], preferred_element_type=F32)
    lane = lax.broadcasted_iota(jnp.int32, ba.shape, 1)
    gate = -jnp.exp(alog_ref[...]) * _softplus(ba + dtb_ref[...])
    bg_ref[...] = jnp.where(lane < DN_HEADS, _sigmoid(ba), gate)

    RC = 64
    for r0 in range(0, R, RC):
        bias = cb_ref[...] + ordering_zeros[min(r0 // RC * len(ordering_zeros) // (R // RC), len(ordering_zeros) - 1)]
        acc = _causal_conv(cext, cw_ref, slice(None), CONF_K, rs, r0, RC) + bias
        mu = jnp.mean(acc, axis=-1, keepdims=True)
        cen = acc - mu
        var = jnp.mean(cen * cen, axis=-1, keepdims=True)
        n = cen * lax.rsqrt(var + EPS) * lng_ref[...] + lnb_ref[...]
        conf_ref[r0:r0 + RC, :] = _silu(n)

    RD = 32
    for r0 in range(0, R, RD):
        y = _silu(_causal_conv(dext, dw_ref, slice(None), DN_CONV_K, rs, r0, RD))
        for j in range(3 * DN_HEADS):
            seg = y[:, j * DN_HEAD_DIM:(j + 1) * DN_HEAD_DIM]
            if j < 2 * DN_HEADS:
                seg = seg * lax.rsqrt(jnp.sum(seg * seg, axis=-1, keepdims=True) + EPS)
            qkv_ref[r0:r0 + RD, j * DN_HEAD_DIM:(j + 1) * DN_HEAD_DIM] = seg

    _halo_end(cext, cbuf_ref, CONF_K, rs, R, t, NT)
    _halo_end(dext, dbuf_ref, DN_CONV_K, rs, R, t, NT)


def _premix(x, cbuf0, dbuf0, pre_w, *, R, rs):
    G, L, _ = x.shape
    NT = L // R
    c_halo, HC = _halo(CONF_K, rs)
    d_halo, HD = _halo(DN_CONV_K, rs)

    def tile(c):
        return pl.BlockSpec((None, R, c), lambda gi, ti: (gi, ti, 0))

    def full(a):
        return pl.BlockSpec(a.shape, lambda gi, ti: (0,) * a.ndim)

    def per_group(rows, c):
        return pl.BlockSpec((None, rows, c), lambda gi, ti: (gi, 0, 0))

    return pl.pallas_call(
        functools.partial(_premix_kernel, R=R, rs=rs, NT=NT),
        grid=(G, NT),
        in_specs=[tile(D_MODEL)] + [full(a) for a in pre_w]
        + [per_group(c_halo, CONF_WIDTH), per_group(d_halo, 3 * DN_WIDTH)],
        out_specs=[tile(CONF_WIDTH), tile(3 * DN_WIDTH), tile(DN_WIDTH), tile(LANES),
                   per_group(c_halo, CONF_WIDTH), per_group(d_halo, 3 * DN_WIDTH)],
        out_shape=[jax.ShapeDtypeStruct((G, L, CONF_WIDTH), F32),
                   jax.ShapeDtypeStruct((G, L, 3 * DN_WIDTH), F32),
                   jax.ShapeDtypeStruct((G, L, DN_WIDTH), F32),
                   jax.ShapeDtypeStruct((G, L, LANES), F32),
                   jax.ShapeDtypeStruct((G, c_halo, CONF_WIDTH), F32),
                   jax.ShapeDtypeStruct((G, d_halo, 3 * DN_WIDTH), F32)],
        scratch_shapes=[pltpu.VMEM((HC + R, CONF_WIDTH), F32), pltpu.VMEM((HD + R, 3 * DN_WIDTH), F32)],
        compiler_params=pltpu.CompilerParams(dimension_semantics=("arbitrary", "arbitrary"),
                                             vmem_limit_bytes=VMEM_LIMIT),
        name="premix",
    )(x, *pre_w, cbuf0, dbuf0)


def _unit_lower_inverse(a, eye, same_block, n, mm):
    assert n >= 2 and n & (n - 1) == 0, n
    t = eye - jnp.where(same_block(2), a, 0.0)
    s = 2
    while s < n:
        a_off = jnp.where(same_block(2 * s) & jnp.logical_not(same_block(s)), a, 0.0)
        t = t - mm(mm(t, a_off), t)
        s *= 2
    return t


def _head_norm_gate(o, z, ng):
    return o * lax.rsqrt(jnp.mean(o * o, axis=-1, keepdims=True) + EPS) * ng * _silu(z)


def _delta_prompt_kernel(qkv_ref, z_ref, bg_ref, ng_ref, s0_ref, o_ref, snew_ref, s_scr, *, SB, R, NT):
    t = pl.program_id(1)
    state_shape = (SB * DN_HEADS, DN_HEAD_DIM, DN_HEAD_DIM)

    @pl.when(t == 0)
    def _():
        s_scr[...] = s0_ref[...].reshape(state_shape)

    ri = lax.broadcasted_iota(jnp.int32, (R, R), 0)
    ci = lax.broadcasted_iota(jnp.int32, (R, R), 1)
    same_chunk = lax.shift_right_logical(ri, 6) == lax.shift_right_logical(ci, 6)
    csum = jnp.where(same_chunk & (ci <= ri), 1.0, 0.0).astype(BF16)
    bg = [bg_ref[b] for b in range(SB)]
    gc_all = [_mm_exact_lhs(csum, bg[b]) for b in range(SB)]
    gc_t = [g.T for g in gc_all]

    ri64 = lax.broadcasted_iota(jnp.int32, (CHUNK, CHUNK), 0)
    ci64 = lax.broadcasted_iota(jnp.int32, (CHUNK, CHUNK), 1)
    causal = ri64 >= ci64
    strict = ri64 > ci64
    eye = (ri64 == ci64).astype(F32)

    def same_block(s):
        shift = s.bit_length() - 1
        return lax.shift_right_logical(ri64, shift) == lax.shift_right_logical(ci64, shift)

    scale = DN_HEAD_DIM ** -0.5
    NC = R // CHUNK
    per_chunk = SB * DN_HEADS
    blocks = [(b, c * CHUNK, h) for c in range(NC) for b in range(SB) for h in range(DN_HEADS)]

    def heads(col0):
        return jnp.stack([qkv_ref[b, r0:r0 + CHUNK, col0 + h * DN_HEAD_DIM:col0 + (h + 1) * DN_HEAD_DIM]
                          for b, r0, h in blocks])

    q = heads(0) * scale
    k = heads(DN_WIDTH)
    v = heads(2 * DN_WIDTH)
    beta = jnp.stack([bg[b][r0:r0 + CHUNK, h:h + 1] for b, r0, h in blocks])
    gcc = jnp.stack([gc_all[b][r0:r0 + CHUNK, DN_HEADS + h:DN_HEADS + h + 1] for b, r0, h in blocks])
    gcr = jnp.stack([gc_t[b][DN_HEADS + h:DN_HEADS + h + 1, r0:r0 + CHUNK] for b, r0, h in blocks])
    glast = gcc[:, CHUNK - 1:CHUNK, :]
    decay = jnp.exp(jnp.where(causal, gcc - gcr, -jnp.inf))
    kb = k * beta
    a = jnp.where(strict, _bmm_nt(kb, k) * decay, 0.0)
    tinv = _unit_lower_inverse(a, eye, same_block, CHUNK, _bmm)
    eg = jnp.exp(gcc)
    uw = _bmm(tinv, jnp.concatenate([v * beta, kb * eg], axis=2))
    u = uw[:, :, :DN_HEAD_DIM]
    w = uw[:, :, DN_HEAD_DIM:]
    qk = jnp.where(causal, _bmm_nt(q, k) * decay, 0.0)
    qd = q * eg
    kd = k * jnp.exp(glast - gcc)
    eglast = jnp.exp(glast)

    s = s_scr[...]
    for c in range(NC):
        sl = slice(c * per_chunk, (c + 1) * per_chunk)
        v_new = u[sl] - _bmm(w[sl], s)
        o = _bmm(qd[sl], s) + _bmm(qk[sl], v_new)
        s = s * eglast[sl] + _bmm_tn(kd[sl], v_new)
        for n, (b, r0, h) in enumerate(blocks[sl]):
            lo = h * DN_HEAD_DIM
            o_ref[b, r0:r0 + CHUNK, lo:lo + DN_HEAD_DIM] = _head_norm_gate(
                o[n], z_ref[b, r0:r0 + CHUNK, lo:lo + DN_HEAD_DIM], ng_ref[...])
    s_scr[...] = s

    @pl.when(t == NT - 1)
    def _():
        snew_ref[...] = s_scr[...].reshape(snew_ref.shape)


def _delta_prompt(qkv, z, bg, ng, s0, *, SB, R):
    B, L, _ = qkv.shape
    NT = L // R

    def tile(c):
        return pl.BlockSpec((SB, R, c), lambda bi, ti: (bi, ti, 0))

    s_spec = pl.BlockSpec((SB, DN_HEADS, DN_HEAD_DIM, DN_HEAD_DIM), lambda bi, ti: (bi, 0, 0, 0))
    return pl.pallas_call(
        functools.partial(_delta_prompt_kernel, SB=SB, R=R, NT=NT),
        grid=(B // SB, NT),
        in_specs=[tile(3 * DN_WIDTH), tile(DN_WIDTH), tile(LANES),
                  pl.BlockSpec(ng.shape, lambda bi, ti: (0, 0)), s_spec],
        out_specs=[tile(DN_WIDTH), s_spec],
        out_shape=[jax.ShapeDtypeStruct((B, L, DN_WIDTH), F32), jax.ShapeDtypeStruct(s0.shape, F32)],
        scratch_shapes=[pltpu.VMEM((SB * DN_HEADS, DN_HEAD_DIM, DN_HEAD_DIM), F32)],
        compiler_params=pltpu.CompilerParams(dimension_semantics=("arbitrary", "arbitrary"),
                                             vmem_limit_bytes=VMEM_LIMIT),
        name="delta_prompt",
    )(qkv, z, bg, ng, s0)


def _delta_sample_kernel(qkv_ref, z_ref, bg_ref, ng_ref, s0_ref, o_ref, snew_ref,
                         w_scr, qd_scr, kd_scr, u_scr, vn_scr, os_scr, *, T, rs):
    R = T * rs
    shift = rs.bit_length() - 1
    bg = bg_ref[...]
    ri = lax.broadcasted_iota(jnp.int32, (R, R), 0)
    ci = lax.broadcasted_iota(jnp.int32, (R, R), 1)
    same_seq = (ri & (rs - 1)) == (ci & (rs - 1))
    ti = lax.shift_right_logical(ri, shift)
    tj = lax.shift_right_logical(ci, shift)
    causal = same_seq & (ti >= tj)
    strict = same_seq & (ti > tj)
    eye = (ri == ci).astype(F32)
    gc_all = _mm_exact_lhs(jnp.where(causal, 1.0, 0.0).astype(BF16), bg)
    gc_t = gc_all.T
    scale = DN_HEAD_DIM ** -0.5
    seq_rows = [pl.ds(s, T, stride=rs) for s in range(rs)]

    def same_steps(s):
        block = s.bit_length() - 1
        return lax.shift_right_logical(ti, block) == lax.shift_right_logical(tj, block)

    for h in range(DN_HEADS):
        lo = h * DN_HEAD_DIM
        q = qkv_ref[:, lo:lo + DN_HEAD_DIM] * scale
        k = qkv_ref[:, DN_WIDTH + lo:DN_WIDTH + lo + DN_HEAD_DIM]
        v = qkv_ref[:, 2 * DN_WIDTH + lo:2 * DN_WIDTH + lo + DN_HEAD_DIM]
        beta = bg[:, h:h + 1]
        gcc = gc_all[:, DN_HEADS + h:DN_HEADS + h + 1]
        gcr = gc_t[DN_HEADS + h:DN_HEADS + h + 1, :]
        glast = gc_all[R - rs:R, DN_HEADS + h:DN_HEADS + h + 1]
        glast_rows = jnp.concatenate([glast] * T, axis=0)
        decay = jnp.exp(jnp.where(causal, gcc - gcr, -jnp.inf))
        kb = k * beta
        a = jnp.where(strict, _mm_nt(kb, k) * decay, 0.0)
        tinv = _unit_lower_inverse(a, eye, same_steps, T, _mm)
        eg = jnp.exp(gcc)
        uw = _mm(tinv, jnp.concatenate([v * beta, kb * eg], axis=1))
        u_scr[...] = uw[:, :DN_HEAD_DIM]
        w_scr[...] = uw[:, DN_HEAD_DIM:]
        qd_scr[...] = q * eg
        kd_scr[...] = k * jnp.exp(glast_rows - gcc)
        qk = jnp.where(causal, _mm_nt(q, k) * decay, 0.0)
        eglast = jnp.exp(glast)
        wq = jnp.stack([jnp.concatenate([w_scr[rows, :], qd_scr[rows, :]], axis=0) for rows in seq_rows])
        us = jnp.stack([u_scr[rows, :] for rows in seq_rows])
        kds = jnp.stack([kd_scr[rows, :] for rows in seq_rows])
        egl = jnp.stack([eglast[s:s + 1, :] for s in range(rs)])
        state = s0_ref[:, h]
        ws = _bmm(wq, state)
        v_new = us - ws[:, :T]
        snew_ref[:, h] = state * egl + _bmm_tn(kds, v_new)
        for s, rows in enumerate(seq_rows):
            vn_scr[rows, :] = v_new[s]
            os_scr[rows, :] = ws[s, T:]
        o = os_scr[...] + _mm(qk, vn_scr[...])
        o_ref[:, lo:lo + DN_HEAD_DIM] = _head_norm_gate(o, z_ref[:, lo:lo + DN_HEAD_DIM], ng_ref[...])


def _delta_sample(qkv, z, bg, ng, s0, *, T, rs):
    G, R, _ = qkv.shape

    def tile(c):
        return pl.BlockSpec((None, R, c), lambda gi: (gi, 0, 0))

    s_spec = pl.BlockSpec((rs, DN_HEADS, DN_HEAD_DIM, DN_HEAD_DIM), lambda gi: (gi, 0, 0, 0))
    return pl.pallas_call(
        functools.partial(_delta_sample_kernel, T=T, rs=rs),
        grid=(G,),
        in_specs=[tile(3 * DN_WIDTH), tile(DN_WIDTH), tile(LANES),
                  pl.BlockSpec(ng.shape, lambda gi: (0, 0)), s_spec],
        out_specs=[tile(DN_WIDTH), s_spec],
        out_shape=[jax.ShapeDtypeStruct((G, R, DN_WIDTH), F32), jax.ShapeDtypeStruct(s0.shape, F32)],
        scratch_shapes=[pltpu.VMEM((R, DN_HEAD_DIM), F32) for _ in range(6)],
        compiler_params=pltpu.CompilerParams(dimension_semantics=("arbitrary",),
                                             vmem_limit_bytes=VMEM_LIMIT),
        name="delta_sample",
    )(qkv, z, bg, ng, s0)


def _postmix_kernel(*refs, R, rs, NT):
    x_ref, conf_ref, o_ref, p_ref, fbuf0_ref = refs[:5]
    wout_ref, gffn_ref, wup_ref, fw_ref, fb_ref, wdown_ref, wple_ref, wpg_ref, gfin_ref = refs[5:5 + N_POST_W]
    y_ref, fbuf_ref, fext, act_scr = refs[5 + N_POST_W:]
    t = pl.program_id(1)
    _, HF = _halo(FFN_CONV_K, rs)
    _halo_begin(fext, fbuf0_ref, FFN_CONV_K, rs, R, t)

    mix = jnp.concatenate([conf_ref[...].astype(BF16), o_ref[...].astype(BF16)], axis=1)
    x1 = x_ref[...] + jnp.dot(mix, wout_ref[...], preferred_element_type=F32)
    hn = _rms(x1, gffn_ref[...]).astype(BF16)
    for j in range(2 * N_FF_CHUNKS):
        c0 = j * FF_COLS
        fext[HF:HF + R, c0:c0 + FF_COLS] = jnp.dot(hn, wup_ref[:, c0:c0 + FF_COLS],
                                                   preferred_element_type=F32)

    RC = 64
    for j in range(N_FF_CHUNKS):
        for r0 in range(0, R, RC):
            halves = []
            for c0 in (j * FF_COLS, D_FF + j * FF_COLS):
                cols = slice(c0, c0 + FF_COLS)
                halves.append(_causal_conv(fext, fw_ref, cols, FFN_CONV_K, rs, r0, RC) + fb_ref[:, cols])
            act_scr[r0:r0 + RC, j * FF_COLS:(j + 1) * FF_COLS] = (_silu(halves[0]) * halves[1]).astype(BF16)

    RB = 128
    for r0 in range(0, R, RB):
        rows = slice(r0, r0 + RB)
        x2 = x1[rows] + jnp.dot(act_scr[rows, :], wdown_ref[...], preferred_element_type=F32)
        ple = jnp.dot(p_ref[rows, :].astype(BF16), wple_ref[...], preferred_element_type=F32)
        x3 = x2 + ple * _sigmoid(jnp.dot(x2.astype(BF16), wpg_ref[...], preferred_element_type=F32))
        y_ref[rows, :] = _rms(x3, gfin_ref[...])

    _halo_end(fext, fbuf_ref, FFN_CONV_K, rs, R, t, NT)


def _postmix(x, conf, o, p, fbuf0, post_w, *, R, rs):
    G, L, _ = x.shape
    NT = L // R
    halo, HF = _halo(FFN_CONV_K, rs)

    def tile(c):
        return pl.BlockSpec((None, R, c), lambda gi, ti: (gi, ti, 0))

    def full(a):
        return pl.BlockSpec(a.shape, lambda gi, ti: (0,) * a.ndim)

    buf_spec = pl.BlockSpec((None, halo, 2 * D_FF), lambda gi, ti: (gi, 0, 0))
    return pl.pallas_call(
        functools.partial(_postmix_kernel, R=R, rs=rs, NT=NT),
        grid=(G, NT),
        in_specs=[tile(D_MODEL), tile(CONF_WIDTH), tile(DN_WIDTH), tile(D_PLE), buf_spec]
        + [full(a) for a in post_w],
        out_specs=[tile(D_MODEL), buf_spec],
        out_shape=[jax.ShapeDtypeStruct((G, L, D_MODEL), F32),
                   jax.ShapeDtypeStruct((G, halo, 2 * D_FF), F32)],
        scratch_shapes=[pltpu.VMEM((HF + R, 2 * D_FF), F32), pltpu.VMEM((R, D_FF), BF16)],
        compiler_params=pltpu.CompilerParams(dimension_semantics=("arbitrary", "arbitrary"),
                                             vmem_limit_bytes=VMEM_LIMIT),
        name="postmix",
    )(x, conf, o, p, fbuf0, *post_w)


def _sublane_replicated(w):
    return jnp.broadcast_to(w[:, None, :], (w.shape[0], SUBLANES, w.shape[1]))


def _interleave(a, rs):
    n, t, c = a.shape
    return a.reshape(n // rs, rs, t, c).transpose(0, 2, 1, 3).reshape(n // rs, t * rs, c)


def _deinterleave(a, rs, t):
    g, _, c = a.shape
    return a.reshape(g, t, rs, c).transpose(0, 2, 1, 3).reshape(g * rs, t, c)


def _trunk(x, p, conf_buf, dn_buf, s0, ffn_buf, pre_w, ng, post_w, *, interleaved):
    seq = x.shape[1]
    if interleaved:
        rs = SAMPLE_RS
        x, p, conf_buf, dn_buf, ffn_buf = (_interleave(a, rs) for a in (x, p, conf_buf, dn_buf, ffn_buf))
        pre_rows = post_rows = seq * rs
    else:
        rs = 1
        pre_rows, post_rows = PREMIX_ROWS, POSTMIX_ROWS
    conf, qkv, z, bg, conf_new, dn_new = _premix(x, conf_buf, dn_buf, pre_w, R=pre_rows, rs=rs)
    if interleaved:
        o, s_new = _delta_sample(qkv, z, bg, ng, s0, T=seq, rs=rs)
    else:
        o, s_new = _delta_prompt(qkv, z, bg, ng, s0, SB=DELTA_SEQS, R=DELTA_ROWS)
    y, ffn_new = _postmix(x, conf, o, p, ffn_buf, post_w, R=post_rows, rs=rs)
    if interleaved:
        y = _deinterleave(y, rs, seq)
        conf_new = _deinterleave(conf_new, rs, CONF_K - 1)
        dn_new = _deinterleave(dn_new, rs, DN_CONV_K - 1)
        ffn_new = _deinterleave(ffn_new, rs, FFN_CONV_K - 1)
    return y, conf_new[None], dn_new[None], s_new[None], ffn_new[None]


def kernel(x_prompt, x_sample, p_prompt, p_sample, state_conf_buf, state_dn_conv_buf, state_dn_S, state_ffn_buf,
           norm_mix_g, w_in, conf_dw_w, conf_dw_b, conf_ln_g, conf_ln_b, dn_conv_w, dn_a_log, dn_dt_bias,
           dn_norm_g, w_out, norm_ffn_g, w_up, ffn_conv_w, ffn_conv_b, w_down, w_ple, w_ple_gate, norm_final_g):
    assert w_in.shape[0] == 1, "single trunk layer"
    gate_pad = ((0, 0), (DN_HEADS, LANES - 2 * DN_HEADS))
    pre_w = (norm_mix_g[0][None], w_in[0].astype(BF16),
             jnp.pad(w_in[0][:, N_MAIN:], ((0, 0), (0, LANES - 2 * DN_HEADS))).astype(BF16),
             _sublane_replicated(conf_dw_w[0]), conf_dw_b[0][None], conf_ln_g[0][None], conf_ln_b[0][None],
             _sublane_replicated(dn_conv_w[0]), jnp.pad(dn_a_log[0][None], gate_pad),
             jnp.pad(dn_dt_bias[0][None], gate_pad))
    ng = dn_norm_g[0][None]
    post_w = (w_out[0].astype(BF16), norm_ffn_g[0][None], w_up[0].astype(BF16),
              _sublane_replicated(ffn_conv_w[0]), ffn_conv_b[0][None], w_down[0].astype(BF16),
              w_ple[0].astype(BF16), w_ple_gate[0].astype(BF16), norm_final_g[None])
    assert len(pre_w) == N_PRE_W and len(post_w) == N_POST_W

    bp, dt = x_prompt.shape[0], x_prompt.dtype
    z_conf = jnp.zeros((bp, CONF_K - 1, CONF_WIDTH), dt)
    z_dn = jnp.zeros((bp, DN_CONV_K - 1, 3 * DN_WIDTH), dt)
    z_s = jnp.zeros((bp, DN_HEADS, DN_HEAD_DIM, DN_HEAD_DIM), dt)
    z_ffn = jnp.zeros((bp, FFN_CONV_K - 1, 2 * D_FF), dt)
    yp, pc, pd, ps, pf = _trunk(x_prompt, p_prompt[0], z_conf, z_dn, z_s, z_ffn, pre_w, ng, post_w,
                                interleaved=False)
    ys, sc, sd, ss, sf = _trunk(x_sample, p_sample[0], state_conf_buf[0], state_dn_conv_buf[0], state_dn_S[0],
                                state_ffn_buf[0], pre_w, ng, post_w, interleaved=True)
    return (yp, ys, pc, pd, ps, pf, sc, sd, ss, sf)
```

```python
import functools

import jax
import jax.numpy as jnp
from jax import lax
from jax.experimental import pallas as pl
from jax.experimental.pallas import tpu as pltpu

F32 = jnp.float32
BF16 = jnp.bfloat16

D_MODEL = 1024
D_PLE = 256
CONF_WIDTH = 512
CONF_K = 31
DN_HEADS = 4
DN_HEAD_DIM = 128
DN_WIDTH = DN_HEADS * DN_HEAD_DIM
DN_CONV_K = 4
D_FF = 2816
FFN_CONV_K = 3
CHUNK = 64
EPS = 1e-6
N_MAIN = 2 * CONF_WIDTH + 4 * DN_WIDTH

SUBLANES = 8
LANES = 128
FF_COLS = 256
N_FF_CHUNKS = D_FF // FF_COLS
SAMPLE_RS = 32
PREMIX_ROWS = 512
DELTA_ROWS = 128
DELTA_SEQS = 8
POSTMIX_ROWS = 512
VMEM_LIMIT = 56 * 1024 * 1024
N_PRE_W = 10
N_POST_W = 9


def _round_up(n, m):
    return (n + m - 1) // m * m


def _mm(a, b):
    return jnp.dot(a.astype(BF16), b.astype(BF16), preferred_element_type=F32)


def _mm_nt(a, b):
    return lax.dot_general(a.astype(BF16), b.astype(BF16), (((1,), (1,)), ((), ())),
                           preferred_element_type=F32)


def _bmm(a, b):
    return jnp.einsum("nij,njk->nik", a.astype(BF16), b.astype(BF16), preferred_element_type=F32)


def _bmm_nt(a, b):
    return jnp.einsum("nid,njd->nij", a.astype(BF16), b.astype(BF16), preferred_element_type=F32)


def _bmm_tn(a, b):
    return jnp.einsum("nck,ncv->nkv", a.astype(BF16), b.astype(BF16), preferred_element_type=F32)


def _split3(x):
    hi = x.astype(BF16)
    r1 = x - hi.astype(F32)
    mid = r1.astype(BF16)
    lo = (r1 - mid.astype(F32)).astype(BF16)
    return hi, mid, lo


def _mm_exact_lhs(m_bf16, x):
    hi, mid, lo = _split3(x)
    dot = functools.partial(jnp.dot, preferred_element_type=F32)
    return dot(m_bf16, hi) + dot(m_bf16, mid) + dot(m_bf16, lo)


def _sigmoid(x):
    return 1.0 / (1.0 + jnp.exp(-x))


def _silu(x):
    return x * _sigmoid(x)


def _softplus(x):
    return jnp.maximum(x, 0.0) + jnp.log1p(jnp.exp(-jnp.abs(x)))


def _rms(x, g):
    return x * lax.rsqrt(jnp.mean(x * x, axis=-1, keepdims=True) + EPS) * g


def _halo(taps, rs):
    rows = (taps - 1) * rs
    return rows, _round_up(rows, SUBLANES)


def _halo_begin(ext, buf0_ref, taps, rs, R, t):
    rows, padded = _halo(taps, rs)

    @pl.when(t == 0)
    def _():
        ext[padded - rows:padded, :] = buf0_ref[...]

    @pl.when(t > 0)
    def _():
        ext[0:padded, :] = ext[R:R + padded, :]


def _halo_end(ext, buf_ref, taps, rs, R, t, NT):
    rows, padded = _halo(taps, rs)

    @pl.when(t == NT - 1)
    def _():
        buf_ref[...] = ext[padded + R - rows:padded + R, :]


def _causal_conv(ext_ref, w_ref, cols, taps, rs, r0, n_rows):
    _, padded = _halo(taps, rs)
    groups = {}
    for k in range(taps):
        off = padded - (taps - 1 - k) * rs
        groups.setdefault(off % SUBLANES, []).append((k, off - off % SUBLANES))
    total = None
    for mis, members in sorted(groups.items()):
        n = n_rows + (SUBLANES if mis else 0)
        part = None
        for k, base in members:
            w_rows = jnp.concatenate([w_ref[k, :, cols]] * (n // SUBLANES), axis=0)
            term = w_rows * ext_ref[base + r0:base + r0 + n, cols]
            part = term if part is None else part + term
        if mis:
            part = part[mis:mis + n_rows]
        total = part if total is None else total + part
    return total


def _premix_kernel(*refs, R, rs, NT):
    x_ref = refs[0]
    g_ref, win_ref, wba_ref, cw_ref, cb_ref, lng_ref, lnb_ref, dw_ref, alog_ref, dtb_ref = refs[1:1 + N_PRE_W]
    cbuf0_ref, dbuf0_ref, conf_ref, qkv_ref, z_ref, bg_ref, cbuf_ref, dbuf_ref, cext, dext = refs[1 + N_PRE_W:]
    t = pl.program_id(1)
    _, HC = _halo(CONF_K, rs)
    _, HD = _halo(DN_CONV_K, rs)
    q0 = 2 * CONF_WIDTH
    z0 = q0 + 3 * DN_WIDTH
    _halo_begin(cext, cbuf0_ref, CONF_K, rs, R, t)
    _halo_begin(dext, dbuf0_ref, DN_CONV_K, rs, R, t)

    h = _rms(x_ref[...], g_ref[...]).astype(BF16)
    RG = 128
    for r0 in range(0, R, RG):
        vg = jnp.dot(h[r0:r0 + RG], win_ref[:, 0:q0], preferred_element_type=F32)
        cext[HC + r0:HC + r0 + RG, :] = vg[:, :CONF_WIDTH] * _sigmoid(vg[:, CONF_WIDTH:])

    def zero_row_after(piece):
        bits = pltpu.bitcast(piece[0:SUBLANES, 0:LANES], jnp.uint32)
        zero = pltpu.bitcast(lax.shift_right_logical(lax.shift_right_logical(bits, jnp.uint32(16)), jnp.uint32(16)), F32)
        return jnp.concatenate([zero[0:1, :]] * (CONF_WIDTH // LANES), axis=1)

    ordering_zeros = []
    for c0 in range(q0, N_MAIN, FF_COLS):
        piece = jnp.dot(h, win_ref[:, c0:c0 + FF_COLS], preferred_element_type=F32)
        if c0 < z0:
            dext[HD:HD + R, c0 - q0:c0 - q0 + FF_COLS] = piece
        else:
            z_ref[:, c0 - z0:c0 - z0 + FF_COLS] = piece
        ordering_zeros.append(zero_row_after(piece))

    ba = jnp.dot(h, wba_ref[...], preferred_element_type=F32)
    lane = lax.broadcasted_iota(jnp.int32, ba.shape, 1)
    gate = -jnp.exp(alog_ref[...]) * _softplus(ba + dtb_ref[...])
    bg_ref[...] = jnp.where(lane < DN_HEADS, _sigmoid(ba), gate)

    RC = 64
    for r0 in range(0, R, RC):
        bias = cb_ref[...] + ordering_zeros[min(r0 // RC * len(ordering_zeros) // (R // RC), len(ordering_zeros) - 1)]
        acc = _causal_conv(cext, cw_ref, slice(None), CONF_K, rs, r0, RC) + bias
        mu = jnp.mean(acc, axis=-1, keepdims=True)
        cen = acc - mu
        var = jnp.mean(cen * cen, axis=-1, keepdims=True)
        n = cen * lax.rsqrt(var + EPS) * lng_ref[...] + lnb_ref[...]
        conf_ref[r0:r0 + RC, :] = _silu(n).astype(BF16)

    RD = 32
    for r0 in range(0, R, RD):
        y = _silu(_causal_conv(dext, dw_ref, slice(None), DN_CONV_K, rs, r0, RD))
        for j in range(3 * DN_HEADS):
            seg = y[:, j * DN_HEAD_DIM:(j + 1) * DN_HEAD_DIM]
            if j < 2 * DN_HEADS:
                seg = seg * lax.rsqrt(jnp.sum(seg * seg, axis=-1, keepdims=True) + EPS)
            qkv_ref[r0:r0 + RD, j * DN_HEAD_DIM:(j + 1) * DN_HEAD_DIM] = seg

    _halo_end(cext, cbuf_ref, CONF_K, rs, R, t, NT)
    _halo_end(dext, dbuf_ref, DN_CONV_K, rs, R, t, NT)


def _premix(x, cbuf0, dbuf0, pre_w, *, R, rs):
    G, L, _ = x.shape
    NT = L // R
    c_halo, HC = _halo(CONF_K, rs)
    d_halo, HD = _halo(DN_CONV_K, rs)

    def tile(c):
        return pl.BlockSpec((None, R, c), lambda gi, ti: (gi, ti, 0))

    def full(a):
        return pl.BlockSpec(a.shape, lambda gi, ti: (0,) * a.ndim)

    def per_group(rows, c):
        return pl.BlockSpec((None, rows, c), lambda gi, ti: (gi, 0, 0))

    return pl.pallas_call(
        functools.partial(_premix_kernel, R=R, rs=rs, NT=NT),
        grid=(G, NT),
        in_specs=[tile(D_MODEL)] + [full(a) for a in pre_w]
        + [per_group(c_halo, CONF_WIDTH), per_group(d_halo, 3 * DN_WIDTH)],
        out_specs=[tile(CONF_WIDTH), tile(3 * DN_WIDTH), tile(DN_WIDTH), tile(LANES),
                   per_group(c_halo, CONF_WIDTH), per_group(d_halo, 3 * DN_WIDTH)],
        out_shape=[jax.ShapeDtypeStruct((G, L, CONF_WIDTH), BF16),
                   jax.ShapeDtypeStruct((G, L, 3 * DN_WIDTH), F32),
                   jax.ShapeDtypeStruct((G, L, DN_WIDTH), F32),
                   jax.ShapeDtypeStruct((G, L, LANES), F32),
                   jax.ShapeDtypeStruct((G, c_halo, CONF_WIDTH), F32),
                   jax.ShapeDtypeStruct((G, d_halo, 3 * DN_WIDTH), F32)],
        scratch_shapes=[pltpu.VMEM((HC + R, CONF_WIDTH), F32), pltpu.VMEM((HD + R, 3 * DN_WIDTH), F32)],
        compiler_params=pltpu.CompilerParams(dimension_semantics=("arbitrary", "arbitrary"),
                                             vmem_limit_bytes=VMEM_LIMIT),
        name="premix",
    )(x, *pre_w, cbuf0, dbuf0)


def _unit_lower_inverse(a, eye, same_block, n, mm):
    assert n >= 2 and n & (n - 1) == 0, n
    t = eye - jnp.where(same_block(2), a, 0.0)
    s = 2
    while s < n:
        a_off = jnp.where(same_block(2 * s) & jnp.logical_not(same_block(s)), a, 0.0)
        t = t - mm(mm(t, a_off), t)
        s *= 2
    return t


def _head_norm_gate(o, z, ng):
    return (o * lax.rsqrt(jnp.mean(o * o, axis=-1, keepdims=True) + EPS) * ng * _silu(z)).astype(BF16)


def _delta_prompt_kernel(qkv_ref, z_ref, bg_ref, ng_ref, s0_ref, o_ref, snew_ref, s_scr, *, SB, R, NT):
    t = pl.program_id(1)
    state_shape = (SB * DN_HEADS, DN_HEAD_DIM, DN_HEAD_DIM)

    @pl.when(t == 0)
    def _():
        s_scr[...] = s0_ref[...].reshape(state_shape)

    ri = lax.broadcasted_iota(jnp.int32, (R, R), 0)
    ci = lax.broadcasted_iota(jnp.int32, (R, R), 1)
    same_chunk = lax.shift_right_logical(ri, 6) == lax.shift_right_logical(ci, 6)
    csum = jnp.where(same_chunk & (ci <= ri), 1.0, 0.0).astype(BF16)
    bg = [bg_ref[b] for b in range(SB)]
    gc_all = [_mm_exact_lhs(csum, bg[b]) for b in range(SB)]
    gc_t = [g.T for g in gc_all]

    ri64 = lax.broadcasted_iota(jnp.int32, (CHUNK, CHUNK), 0)
    ci64 = lax.broadcasted_iota(jnp.int32, (CHUNK, CHUNK), 1)
    causal = ri64 >= ci64
    strict = ri64 > ci64
    eye = (ri64 == ci64).astype(F32)

    def same_block(s):
        shift = s.bit_length() - 1
        return lax.shift_right_logical(ri64, shift) == lax.shift_right_logical(ci64, shift)

    scale = DN_HEAD_DIM ** -0.5
    NC = R // CHUNK
    per_chunk = SB * DN_HEADS
    blocks = [(b, c * CHUNK, h) for c in range(NC) for b in range(SB) for h in range(DN_HEADS)]

    def heads(col0):
        return jnp.stack([qkv_ref[b, r0:r0 + CHUNK, col0 + h * DN_HEAD_DIM:col0 + (h + 1) * DN_HEAD_DIM]
                          for b, r0, h in blocks])

    q = heads(0) * scale
    k = heads(DN_WIDTH)
    v = heads(2 * DN_WIDTH)
    beta = jnp.stack([bg[b][r0:r0 + CHUNK, h:h + 1] for b, r0, h in blocks])
    gcc = jnp.stack([gc_all[b][r0:r0 + CHUNK, DN_HEADS + h:DN_HEADS + h + 1] for b, r0, h in blocks])
    gcr = jnp.stack([gc_t[b][DN_HEADS + h:DN_HEADS + h + 1, r0:r0 + CHUNK] for b, r0, h in blocks])
    glast = gcc[:, CHUNK - 1:CHUNK, :]
    decay = jnp.exp(jnp.where(causal, gcc - gcr, -jnp.inf))
    kb = k * beta
    a = jnp.where(strict, _bmm_nt(kb, k) * decay, 0.0)
    tinv = _unit_lower_inverse(a, eye, same_block, CHUNK, _bmm)
    eg = jnp.exp(gcc)
    uw = _bmm(tinv, jnp.concatenate([v * beta, kb * eg], axis=2))
    u = uw[:, :, :DN_HEAD_DIM]
    w = uw[:, :, DN_HEAD_DIM:]
    qk = jnp.where(causal, _bmm_nt(q, k) * decay, 0.0)
    qd = q * eg
    kd = k * jnp.exp(glast - gcc)
    eglast = jnp.exp(glast)

    s = s_scr[...]
    for c in range(NC):
        sl = slice(c * per_chunk, (c + 1) * per_chunk)
        v_new = u[sl] - _bmm(w[sl], s)
        o = _bmm(qd[sl], s) + _bmm(qk[sl], v_new)
        s = s * eglast[sl] + _bmm_tn(kd[sl], v_new)
        for n, (b, r0, h) in enumerate(blocks[sl]):
            lo = h * DN_HEAD_DIM
            o_ref[b, r0:r0 + CHUNK, lo:lo + DN_HEAD_DIM] = _head_norm_gate(
                o[n], z_ref[b, r0:r0 + CHUNK, lo:lo + DN_HEAD_DIM], ng_ref[...])
    s_scr[...] = s

    @pl.when(t == NT - 1)
    def _():
        snew_ref[...] = s_scr[...].reshape(snew_ref.shape)


def _delta_prompt(qkv, z, bg, ng, s0, *, SB, R):
    B, L, _ = qkv.shape
    NT = L // R

    def tile(c):
        return pl.BlockSpec((SB, R, c), lambda bi, ti: (bi, ti, 0))

    s_spec = pl.BlockSpec((SB, DN_HEADS, DN_HEAD_DIM, DN_HEAD_DIM), lambda bi, ti: (bi, 0, 0, 0))
    return pl.pallas_call(
        functools.partial(_delta_prompt_kernel, SB=SB, R=R, NT=NT),
        grid=(B // SB, NT),
        in_specs=[tile(3 * DN_WIDTH), tile(DN_WIDTH), tile(LANES),
                  pl.BlockSpec(ng.shape, lambda bi, ti: (0, 0)), s_spec],
        out_specs=[tile(DN_WIDTH), s_spec],
        out_shape=[jax.ShapeDtypeStruct((B, L, DN_WIDTH), BF16), jax.ShapeDtypeStruct(s0.shape, F32)],
        scratch_shapes=[pltpu.VMEM((SB * DN_HEADS, DN_HEAD_DIM, DN_HEAD_DIM), F32)],
        compiler_params=pltpu.CompilerParams(dimension_semantics=("arbitrary", "arbitrary"),
                                             vmem_limit_bytes=VMEM_LIMIT),
        name="delta_prompt",
    )(qkv, z, bg, ng, s0)


def _delta_sample_kernel(qkv_ref, z_ref, bg_ref, ng_ref, s0_ref, o_ref, snew_ref,
                         w_scr, qd_scr, kd_scr, u_scr, vn_scr, os_scr, *, T, rs):
    R = T * rs
    shift = rs.bit_length() - 1
    bg = bg_ref[...]
    ri = lax.broadcasted_iota(jnp.int32, (R, R), 0)
    ci = lax.broadcasted_iota(jnp.int32, (R, R), 1)
    same_seq = (ri & (rs - 1)) == (ci & (rs - 1))
    ti = lax.shift_right_logical(ri, shift)
    tj = lax.shift_right_logical(ci, shift)
    causal = same_seq & (ti >= tj)
    strict = same_seq & (ti > tj)
    eye = (ri == ci).astype(F32)
    gc_all = _mm_exact_lhs(jnp.where(causal, 1.0, 0.0).astype(BF16), bg)
    gc_t = gc_all.T
    scale = DN_HEAD_DIM ** -0.5
    seq_rows = [pl.ds(s, T, stride=rs) for s in range(rs)]

    def same_steps(s):
        block = s.bit_length() - 1
        return lax.shift_right_logical(ti, block) == lax.shift_right_logical(tj, block)

    for h in range(DN_HEADS):
        lo = h * DN_HEAD_DIM
        q = qkv_ref[:, lo:lo + DN_HEAD_DIM] * scale
        k = qkv_ref[:, DN_WIDTH + lo:DN_WIDTH + lo + DN_HEAD_DIM]
        v = qkv_ref[:, 2 * DN_WIDTH + lo:2 * DN_WIDTH + lo + DN_HEAD_DIM]
        beta = bg[:, h:h + 1]
        gcc = gc_all[:, DN_HEADS + h:DN_HEADS + h + 1]
        gcr = gc_t[DN_HEADS + h:DN_HEADS + h + 1, :]
        glast = gc_all[R - rs:R, DN_HEADS + h:DN_HEADS + h + 1]
        glast_rows = jnp.concatenate([glast] * T, axis=0)
        decay = jnp.exp(jnp.where(causal, gcc - gcr, -jnp.inf))
        kb = k * beta
        a = jnp.where(strict, _mm_nt(kb, k) * decay, 0.0)
        tinv = _unit_lower_inverse(a, eye, same_steps, T, _mm)
        eg = jnp.exp(gcc)
        uw = _mm(tinv, jnp.concatenate([v * beta, kb * eg], axis=1))
        u_scr[...] = uw[:, :DN_HEAD_DIM]
        w_scr[...] = uw[:, DN_HEAD_DIM:]
        qd_scr[...] = q * eg
        kd_scr[...] = k * jnp.exp(glast_rows - gcc)
        qk = jnp.where(causal, _mm_nt(q, k) * decay, 0.0)
        eglast = jnp.exp(glast)
        wq = jnp.stack([jnp.concatenate([w_scr[rows, :], qd_scr[rows, :]], axis=0) for rows in seq_rows])
        us = jnp.stack([u_scr[rows, :] for rows in seq_rows])
        kds = jnp.stack([kd_scr[rows, :] for rows in seq_rows])
        egl = jnp.stack([eglast[s:s + 1, :] for s in range(rs)])
        state = s0_ref[:, h]
        ws = _bmm(wq, state)
        v_new = us - ws[:, :T]
        snew_ref[:, h] = state * egl + _bmm_tn(kds, v_new)
        for s, rows in enumerate(seq_rows):
            vn_scr[rows, :] = v_new[s]
            os_scr[rows, :] = ws[s, T:]
        o = os_scr[...] + _mm(qk, vn_scr[...])
        o_ref[:, lo:lo + DN_HEAD_DIM] = _head_norm_gate(o, z_ref[:, lo:lo + DN_HEAD_DIM], ng_ref[...])


def _delta_sample(qkv, z, bg, ng, s0, *, T, rs):
    G, R, _ = qkv.shape

    def tile(c):
        return pl.BlockSpec((None, R, c), lambda gi: (gi, 0, 0))

    s_spec = pl.BlockSpec((rs, DN_HEADS, DN_HEAD_DIM, DN_HEAD_DIM), lambda gi: (gi, 0, 0, 0))
    return pl.pallas_call(
        functools.partial(_delta_sample_kernel, T=T, rs=rs),
        grid=(G,),
        in_specs=[tile(3 * DN_WIDTH), tile(DN_WIDTH), tile(LANES),
                  pl.BlockSpec(ng.shape, lambda gi: (0, 0)), s_spec],
        out_specs=[tile(DN_WIDTH), s_spec],
        out_shape=[jax.ShapeDtypeStruct((G, R, DN_WIDTH), BF16), jax.ShapeDtypeStruct(s0.shape, F32)],
        scratch_shapes=[pltpu.VMEM((R, DN_HEAD_DIM), F32) for _ in range(6)],
        compiler_params=pltpu.CompilerParams(dimension_semantics=("arbitrary",),
                                             vmem_limit_bytes=VMEM_LIMIT),
        name="delta_sample",
    )(qkv, z, bg, ng, s0)


def _postmix_kernel(*refs, R, rs, NT):
    x_ref, conf_ref, o_ref, p_ref, fbuf0_ref = refs[:5]
    wout_ref, gffn_ref, wup_ref, fw_ref, fb_ref, wdown_ref, wple_ref, wpg_ref, gfin_ref = refs[5:5 + N_POST_W]
    y_ref, fbuf_ref, fext, act_scr = refs[5 + N_POST_W:]
    t = pl.program_id(1)
    _, HF = _halo(FFN_CONV_K, rs)
    _halo_begin(fext, fbuf0_ref, FFN_CONV_K, rs, R, t)

    mix = jnp.concatenate([conf_ref[...].astype(BF16), o_ref[...].astype(BF16)], axis=1)
    x1 = x_ref[...] + jnp.dot(mix, wout_ref[...], preferred_element_type=F32)
    hn = _rms(x1, gffn_ref[...]).astype(BF16)
    for j in range(2 * N_FF_CHUNKS):
        c0 = j * FF_COLS
        fext[HF:HF + R, c0:c0 + FF_COLS] = jnp.dot(hn, wup_ref[:, c0:c0 + FF_COLS],
                                                   preferred_element_type=F32)

    RC = 64
    for j in range(N_FF_CHUNKS):
        for r0 in range(0, R, RC):
            halves = []
            for c0 in (j * FF_COLS, D_FF + j * FF_COLS):
                cols = slice(c0, c0 + FF_COLS)
                halves.append(_causal_conv(fext, fw_ref, cols, FFN_CONV_K, rs, r0, RC) + fb_ref[:, cols])
            act_scr[r0:r0 + RC, j * FF_COLS:(j + 1) * FF_COLS] = (_silu(halves[0]) * halves[1]).astype(BF16)

    RB = 128
    for r0 in range(0, R, RB):
        rows = slice(r0, r0 + RB)
        x2 = x1[rows] + jnp.dot(act_scr[rows, :], wdown_ref[...], preferred_element_type=F32)
        ple = jnp.dot(p_ref[rows, :].astype(BF16), wple_ref[...], preferred_element_type=F32)
        x3 = x2 + ple * _sigmoid(jnp.dot(x2.astype(BF16), wpg_ref[...], preferred_element_type=F32))
        y_ref[rows, :] = _rms(x3, gfin_ref[...])

    _halo_end(fext, fbuf_ref, FFN_CONV_K, rs, R, t, NT)


def _postmix(x, conf, o, p, fbuf0, post_w, *, R, rs):
    G, L, _ = x.shape
    NT = L // R
    halo, HF = _halo(FFN_CONV_K, rs)

    def tile(c):
        return pl.BlockSpec((None, R, c), lambda gi, ti: (gi, ti, 0))

    def full(a):
        return pl.BlockSpec(a.shape, lambda gi, ti: (0,) * a.ndim)

    buf_spec = pl.BlockSpec((None, halo, 2 * D_FF), lambda gi, ti: (gi, 0, 0))
    return pl.pallas_call(
        functools.partial(_postmix_kernel, R=R, rs=rs, NT=NT),
        grid=(G, NT),
        in_specs=[tile(D_MODEL), tile(CONF_WIDTH), tile(DN_WIDTH), tile(D_PLE), buf_spec]
        + [full(a) for a in post_w],
        out_specs=[tile(D_MODEL), buf_spec],
        out_shape=[jax.ShapeDtypeStruct((G, L, D_MODEL), F32),
                   jax.ShapeDtypeStruct((G, halo, 2 * D_FF), F32)],
        scratch_shapes=[pltpu.VMEM((HF + R, 2 * D_FF), F32), pltpu.VMEM((R, D_FF), BF16)],
        compiler_params=pltpu.CompilerParams(dimension_semantics=("arbitrary", "arbitrary"),
                                             vmem_limit_bytes=VMEM_LIMIT),
        name="postmix",
    )(x, conf, o, p, fbuf0, *post_w)


def _sublane_replicated(w):
    return jnp.broadcast_to(w[:, None, :], (w.shape[0], SUBLANES, w.shape[1]))


def _interleave(a, rs):
    n, t, c = a.shape
    return a.reshape(n // rs, rs, t, c).transpose(0, 2, 1, 3).reshape(n // rs, t * rs, c)


def _deinterleave(a, rs, t):
    g, _, c = a.shape
    return a.reshape(g, t, rs, c).transpose(0, 2, 1, 3).reshape(g * rs, t, c)


def _trunk(x, p, conf_buf, dn_buf, s0, ffn_buf, pre_w, ng, post_w, *, interleaved):
    seq = x.shape[1]
    if interleaved:
        rs = SAMPLE_RS
        x, p, conf_buf, dn_buf, ffn_buf = (_interleave(a, rs) for a in (x, p, conf_buf, dn_buf, ffn_buf))
        pre_rows = post_rows = seq * rs
    else:
        rs = 1
        pre_rows, post_rows = PREMIX_ROWS, POSTMIX_ROWS
    conf, qkv, z, bg, conf_new, dn_new = _premix(x, conf_buf, dn_buf, pre_w, R=pre_rows, rs=rs)
    if interleaved:
        o, s_new = _delta_sample(qkv, z, bg, ng, s0, T=seq, rs=rs)
    else:
        o, s_new = _delta_prompt(qkv, z, bg, ng, s0, SB=DELTA_SEQS, R=DELTA_ROWS)
    y, ffn_new = _postmix(x, conf, o, p, ffn_buf, post_w, R=post_rows, rs=rs)
    if interleaved:
        y = _deinterleave(y, rs, seq)
        conf_new = _deinterleave(conf_new, rs, CONF_K - 1)
        dn_new = _deinterleave(dn_new, rs, DN_CONV_K - 1)
        ffn_new = _deinterleave(ffn_new, rs, FFN_CONV_K - 1)
    return y, conf_new[None], dn_new[None], s_new[None], ffn_new[None]


def kernel(x_prompt, x_sample, p_prompt, p_sample, state_conf_buf, state_dn_conv_buf, state_dn_S, state_ffn_buf,
           norm_mix_g, w_in, conf_dw_w, conf_dw_b, conf_ln_g, conf_ln_b, dn_conv_w, dn_a_log, dn_dt_bias,
           dn_norm_g, w_out, norm_ffn_g, w_up, ffn_conv_w, ffn_conv_b, w_down, w_ple, w_ple_gate, norm_final_g):
    assert w_in.shape[0] == 1, "single trunk layer"
    gate_pad = ((0, 0), (DN_HEADS, LANES - 2 * DN_HEADS))
    pre_w = (norm_mix_g[0][None], w_in[0].astype(BF16),
             jnp.pad(w_in[0][:, N_MAIN:], ((0, 0), (0, LANES - 2 * DN_HEADS))).astype(BF16),
             _sublane_replicated(conf_dw_w[0]), conf_dw_b[0][None], conf_ln_g[0][None], conf_ln_b[0][None],
             _sublane_replicated(dn_conv_w[0]), jnp.pad(dn_a_log[0][None], gate_pad),
             jnp.pad(dn_dt_bias[0][None], gate_pad))
    ng = dn_norm_g[0][None]
    post_w = (w_out[0].astype(BF16), norm_ffn_g[0][None], w_up[0].astype(BF16),
              _sublane_replicated(ffn_conv_w[0]), ffn_conv_b[0][None], w_down[0].astype(BF16),
              w_ple[0].astype(BF16), w_ple_gate[0].astype(BF16), norm_final_g[None])
    assert len(pre_w) == N_PRE_W and len(post_w) == N_POST_W

    bp, dt = x_prompt.shape[0], x_prompt.dtype
    z_conf = jnp.zeros((bp, CONF_K - 1, CONF_WIDTH), dt)
    z_dn = jnp.zeros((bp, DN_CONV_K - 1, 3 * DN_WIDTH), dt)
    z_s = jnp.zeros((bp, DN_HEADS, DN_HEAD_DIM, DN_HEAD_DIM), dt)
    z_ffn = jnp.zeros((bp, FFN_CONV_K - 1, 2 * D_FF), dt)
    yp, pc, pd, ps, pf = _trunk(x_prompt, p_prompt[0], z_conf, z_dn, z_s, z_ffn, pre_w, ng, post_w,
                                interleaved=False)
    ys, sc, sd, ss, sf = _trunk(x_sample, p_sample[0], state_conf_buf[0], state_dn_conv_buf[0], state_dn_S[0],
                                state_ffn_buf[0], pre_w, ng, post_w, interleaved=True)
    return (yp, ys, pc, pd, ps, pf, sc, sd, ss, sf)
```

```python
import functools

import jax
import jax.numpy as jnp
from jax import lax
from jax.experimental import pallas as pl
from jax.experimental.pallas import tpu as pltpu

F32 = jnp.float32
BF16 = jnp.bfloat16

D_MODEL = 1024
D_PLE = 256
CONF_WIDTH = 512
CONF_K = 31
DN_HEADS = 4
DN_HEAD_DIM = 128
DN_WIDTH = DN_HEADS * DN_HEAD_DIM
DN_CONV_K = 4
D_FF = 2816
FFN_CONV_K = 3
CHUNK = 64
EPS = 1e-6
N_MAIN = 2 * CONF_WIDTH + 4 * DN_WIDTH

SUBLANES = 8
LANES = 128
FF_COLS = 256
N_FF_CHUNKS = D_FF // FF_COLS
SAMPLE_RS = 32
PREMIX_ROWS = 512
DELTA_ROWS = 128
DELTA_SEQS = 8
POSTMIX_ROWS = 512
VMEM_LIMIT = 56 * 1024 * 1024
N_PRE_W = 10
N_POST_W = 9


def _round_up(n, m):
    return (n + m - 1) // m * m


def _mm(a, b):
    return jnp.dot(a.astype(BF16), b.astype(BF16), preferred_element_type=F32)


def _mm_nt(a, b):
    return lax.dot_general(a.astype(BF16), b.astype(BF16), (((1,), (1,)), ((), ())),
                           preferred_element_type=F32)


def _bmm(a, b):
    return jnp.einsum("nij,njk->nik", a.astype(BF16), b.astype(BF16), preferred_element_type=F32)


def _bmm_nt(a, b):
    return jnp.einsum("nid,njd->nij", a.astype(BF16), b.astype(BF16), preferred_element_type=F32)


def _bmm_tn(a, b):
    return jnp.einsum("nck,ncv->nkv", a.astype(BF16), b.astype(BF16), preferred_element_type=F32)


def _split3(x):
    hi = x.astype(BF16)
    r1 = x - hi.astype(F32)
    mid = r1.astype(BF16)
    lo = (r1 - mid.astype(F32)).astype(BF16)
    return hi, mid, lo


def _mm_exact_lhs(m_bf16, x):
    hi, mid, lo = _split3(x)
    dot = functools.partial(jnp.dot, preferred_element_type=F32)
    return dot(m_bf16, hi) + dot(m_bf16, mid) + dot(m_bf16, lo)


def _sigmoid(x):
    return 1.0 / (1.0 + jnp.exp(-x))


def _silu(x):
    return x * _sigmoid(x)


def _softplus(x):
    return jnp.maximum(x, 0.0) + jnp.log1p(jnp.exp(-jnp.abs(x)))


def _rms(x, g):
    return x * lax.rsqrt(jnp.mean(x * x, axis=-1, keepdims=True) + EPS) * g


def _halo(taps, rs):
    rows = (taps - 1) * rs
    return rows, _round_up(rows, SUBLANES)


def _halo_begin(ext, buf0_ref, taps, rs, R, t):
    rows, padded = _halo(taps, rs)

    @pl.when(t == 0)
    def _():
        ext[padded - rows:padded, :] = buf0_ref[...]

    @pl.when(t > 0)
    def _():
        ext[0:padded, :] = ext[R:R + padded, :]


def _halo_end(ext, buf_ref, taps, rs, R, t, NT):
    rows, padded = _halo(taps, rs)

    @pl.when(t == NT - 1)
    def _():
        buf_ref[...] = ext[padded + R - rows:padded + R, :]


def _causal_conv(ext_ref, w_ref, cols, taps, rs, r0, n_rows):
    _, padded = _halo(taps, rs)
    groups = {}
    for k in range(taps):
        off = padded - (taps - 1 - k) * rs
        groups.setdefault(off % SUBLANES, []).append((k, off - off % SUBLANES))
    total = None
    for mis, members in sorted(groups.items()):
        n = n_rows + (SUBLANES if mis else 0)
        part = None
        for k, base in members:
            w_rows = jnp.concatenate([w_ref[k, :, cols]] * (n // SUBLANES), axis=0)
            term = w_rows * ext_ref[base + r0:base + r0 + n, cols]
            part = term if part is None else part + term
        if mis:
            part = part[mis:mis + n_rows]
        total = part if total is None else total + part
    return total


def _premix_kernel(*refs, R, rs, NT):
    x_ref = refs[0]
    g_ref, win_ref, wba_ref, cw_ref, cb_ref, lng_ref, lnb_ref, dw_ref, alog_ref, dtb_ref = refs[1:1 + N_PRE_W]
    cbuf0_ref, dbuf0_ref, conf_ref, qkv_ref, z_ref, bg_ref, cbuf_ref, dbuf_ref, cext, dext = refs[1 + N_PRE_W:]
    t = pl.program_id(1)
    _, HC = _halo(CONF_K, rs)
    _, HD = _halo(DN_CONV_K, rs)
    q0 = 2 * CONF_WIDTH
    z0 = q0 + 3 * DN_WIDTH
    _halo_begin(cext, cbuf0_ref, CONF_K, rs, R, t)
    _halo_begin(dext, dbuf0_ref, DN_CONV_K, rs, R, t)

    h = _rms(x_ref[...], g_ref[...]).astype(BF16)
    RG = 128
    for r0 in range(0, R, RG):
        vg = jnp.dot(h[r0:r0 + RG], win_ref[:, 0:q0], preferred_element_type=F32)
        cext[HC + r0:HC + r0 + RG, :] = vg[:, :CONF_WIDTH] * _sigmoid(vg[:, CONF_WIDTH:])

    def zero_row_after(piece):
        bits = pltpu.bitcast(piece[0:SUBLANES, 0:LANES], jnp.uint32)
        zero = pltpu.bitcast(lax.shift_right_logical(lax.shift_right_logical(bits, jnp.uint32(16)), jnp.uint32(16)), F32)
        return jnp.concatenate([zero[0:1, :]] * (CONF_WIDTH // LANES), axis=1)

    ordering_zeros = []
    for c0 in range(q0, N_MAIN, FF_COLS):
        piece = jnp.dot(h, win_ref[:, c0:c0 + FF_COLS], preferred_element_type=F32)
        if c0 < z0:
            dext[HD:HD + R, c0 - q0:c0 - q0 + FF_COLS] = piece
        else:
            z_ref[:, c0 - z0:c0 - z0 + FF_COLS] = piece
        ordering_zeros.append(zero_row_after(piece))

    ba = jnp.dot(h, wba_ref[...], preferred_element_type=F32)
    lane = lax.broadcasted_iota(jnp.int32, ba.shape, 1)
    gate = -jnp.exp(alog_ref[...]) * _softplus(ba + dtb_ref[...])
    bg_ref[...] = jnp.where(lane < DN_HEADS, _sigmoid(ba), gate)

    RC = 64
    for r0 in range(0, R, RC):
        bias = cb_ref[...] + ordering_zeros[min(r0 // RC * len(ordering_zeros) // (R // RC), len(ordering_zeros) - 1)]
        acc = _causal_conv(cext, cw_ref, slice(None), CONF_K, rs, r0, RC) + bias
        mu = jnp.mean(acc, axis=-1, keepdims=True)
        cen = acc - mu
        var = jnp.mean(cen * cen, axis=-1, keepdims=True)
        n = cen * lax.rsqrt(var + EPS) * lng_ref[...] + lnb_ref[...]
        conf_ref[r0:r0 + RC, :] = _silu(n).astype(BF16)

    RD = 32
    for r0 in range(0, R, RD):
        qkv_ref[r0:r0 + RD, :] = _causal_conv(dext, dw_ref, slice(None), DN_CONV_K, rs, r0, RD)

    _halo_end(cext, cbuf_ref, CONF_K, rs, R, t, NT)
    _halo_end(dext, dbuf_ref, DN_CONV_K, rs, R, t, NT)


def _premix(x, cbuf0, dbuf0, pre_w, *, R, rs):
    G, L, _ = x.shape
    NT = L // R
    c_halo, HC = _halo(CONF_K, rs)
    d_halo, HD = _halo(DN_CONV_K, rs)

    def tile(c):
        return pl.BlockSpec((None, R, c), lambda gi, ti: (gi, ti, 0))

    def full(a):
        return pl.BlockSpec(a.shape, lambda gi, ti: (0,) * a.ndim)

    def per_group(rows, c):
        return pl.BlockSpec((None, rows, c), lambda gi, ti: (gi, 0, 0))

    return pl.pallas_call(
        functools.partial(_premix_kernel, R=R, rs=rs, NT=NT),
        grid=(G, NT),
        in_specs=[tile(D_MODEL)] + [full(a) for a in pre_w]
        + [per_group(c_halo, CONF_WIDTH), per_group(d_halo, 3 * DN_WIDTH)],
        out_specs=[tile(CONF_WIDTH), tile(3 * DN_WIDTH), tile(DN_WIDTH), tile(LANES),
                   per_group(c_halo, CONF_WIDTH), per_group(d_halo, 3 * DN_WIDTH)],
        out_shape=[jax.ShapeDtypeStruct((G, L, CONF_WIDTH), BF16),
                   jax.ShapeDtypeStruct((G, L, 3 * DN_WIDTH), F32),
                   jax.ShapeDtypeStruct((G, L, DN_WIDTH), F32),
                   jax.ShapeDtypeStruct((G, L, LANES), F32),
                   jax.ShapeDtypeStruct((G, c_halo, CONF_WIDTH), F32),
                   jax.ShapeDtypeStruct((G, d_halo, 3 * DN_WIDTH), F32)],
        scratch_shapes=[pltpu.VMEM((HC + R, CONF_WIDTH), F32), pltpu.VMEM((HD + R, 3 * DN_WIDTH), F32)],
        compiler_params=pltpu.CompilerParams(dimension_semantics=("arbitrary", "arbitrary"),
                                             vmem_limit_bytes=VMEM_LIMIT),
        name="premix",
    )(x, *pre_w, cbuf0, dbuf0)


def _unit_lower_inverse(a, eye, same_block, n, mm):
    assert n >= 2 and n & (n - 1) == 0, n
    t = eye - jnp.where(same_block(2), a, 0.0)
    s = 2
    while s < n:
        a_off = jnp.where(same_block(2 * s) & jnp.logical_not(same_block(s)), a, 0.0)
        t = t - mm(mm(t, a_off), t)
        s *= 2
    return t


def _dn_activation(x, normalize):
    y = _silu(x)
    if normalize:
        y = y * lax.rsqrt(jnp.sum(y * y, axis=-1, keepdims=True) + EPS)
    return y


def _head_norm_gate(o, z, ng):
    return (o * lax.rsqrt(jnp.mean(o * o, axis=-1, keepdims=True) + EPS) * ng * _silu(z)).astype(BF16)


def _delta_prompt_kernel(qkv_ref, z_ref, bg_ref, ng_ref, s0_ref, o_ref, snew_ref, s_scr, *, SB, R, NT):
    t = pl.program_id(1)
    state_shape = (SB * DN_HEADS, DN_HEAD_DIM, DN_HEAD_DIM)

    @pl.when(t == 0)
    def _():
        s_scr[...] = s0_ref[...].reshape(state_shape)

    ri = lax.broadcasted_iota(jnp.int32, (R, R), 0)
    ci = lax.broadcasted_iota(jnp.int32, (R, R), 1)
    same_chunk = lax.shift_right_logical(ri, 6) == lax.shift_right_logical(ci, 6)
    csum = jnp.where(same_chunk & (ci <= ri), 1.0, 0.0).astype(BF16)
    bg = [bg_ref[b] for b in range(SB)]
    gc_all = [_mm_exact_lhs(csum, bg[b]) for b in range(SB)]
    gc_t = [g.T for g in gc_all]

    ri64 = lax.broadcasted_iota(jnp.int32, (CHUNK, CHUNK), 0)
    ci64 = lax.broadcasted_iota(jnp.int32, (CHUNK, CHUNK), 1)
    causal = ri64 >= ci64
    strict = ri64 > ci64
    eye = (ri64 == ci64).astype(F32)

    def same_block(s):
        shift = s.bit_length() - 1
        return lax.shift_right_logical(ri64, shift) == lax.shift_right_logical(ci64, shift)

    scale = DN_HEAD_DIM ** -0.5
    NC = R // CHUNK
    per_chunk = SB * DN_HEADS
    blocks = [(b, c * CHUNK, h) for c in range(NC) for b in range(SB) for h in range(DN_HEADS)]

    def heads(col0):
        return jnp.stack([qkv_ref[b, r0:r0 + CHUNK, col0 + h * DN_HEAD_DIM:col0 + (h + 1) * DN_HEAD_DIM]
                          for b, r0, h in blocks])

    q = _dn_activation(heads(0), True) * scale
    k = _dn_activation(heads(DN_WIDTH), True)
    v = _dn_activation(heads(2 * DN_WIDTH), False)
    beta = jnp.stack([bg[b][r0:r0 + CHUNK, h:h + 1] for b, r0, h in blocks])
    gcc = jnp.stack([gc_all[b][r0:r0 + CHUNK, DN_HEADS + h:DN_HEADS + h + 1] for b, r0, h in blocks])
    gcr = jnp.stack([gc_t[b][DN_HEADS + h:DN_HEADS + h + 1, r0:r0 + CHUNK] for b, r0, h in blocks])
    glast = gcc[:, CHUNK - 1:CHUNK, :]
    decay = jnp.exp(jnp.where(causal, gcc - gcr, -jnp.inf))
    kb = k * beta
    a = jnp.where(strict, _bmm_nt(kb, k) * decay, 0.0)
    tinv = _unit_lower_inverse(a, eye, same_block, CHUNK, _bmm)
    eg = jnp.exp(gcc)
    uw = _bmm(tinv, jnp.concatenate([v * beta, kb * eg], axis=2))
    u = uw[:, :, :DN_HEAD_DIM]
    w = uw[:, :, DN_HEAD_DIM:]
    qk = jnp.where(causal, _bmm_nt(q, k) * decay, 0.0)
    qd = q * eg
    kd = k * jnp.exp(glast - gcc)
    eglast = jnp.exp(glast)

    s = s_scr[...]
    for c in range(NC):
        sl = slice(c * per_chunk, (c + 1) * per_chunk)
        v_new = u[sl] - _bmm(w[sl], s)
        o = _bmm(qd[sl], s) + _bmm(qk[sl], v_new)
        s = s * eglast[sl] + _bmm_tn(kd[sl], v_new)
        for n, (b, r0, h) in enumerate(blocks[sl]):
            lo = h * DN_HEAD_DIM
            o_ref[b, r0:r0 + CHUNK, lo:lo + DN_HEAD_DIM] = _head_norm_gate(
                o[n], z_ref[b, r0:r0 + CHUNK, lo:lo + DN_HEAD_DIM], ng_ref[...])
    s_scr[...] = s

    @pl.when(t == NT - 1)
    def _():
        snew_ref[...] = s_scr[...].reshape(snew_ref.shape)


def _delta_prompt(qkv, z, bg, ng, s0, *, SB, R):
    B, L, _ = qkv.shape
    NT = L // R

    def tile(c):
        return pl.BlockSpec((SB, R, c), lambda bi, ti: (bi, ti, 0))

    s_spec = pl.BlockSpec((SB, DN_HEADS, DN_HEAD_DIM, DN_HEAD_DIM), lambda bi, ti: (bi, 0, 0, 0))
    return pl.pallas_call(
        functools.partial(_delta_prompt_kernel, SB=SB, R=R, NT=NT),
        grid=(B // SB, NT),
        in_specs=[tile(3 * DN_WIDTH), tile(DN_WIDTH), tile(LANES),
                  pl.BlockSpec(ng.shape, lambda bi, ti: (0, 0)), s_spec],
        out_specs=[tile(DN_WIDTH), s_spec],
        out_shape=[jax.ShapeDtypeStruct((B, L, DN_WIDTH), BF16), jax.ShapeDtypeStruct(s0.shape, F32)],
        scratch_shapes=[pltpu.VMEM((SB * DN_HEADS, DN_HEAD_DIM, DN_HEAD_DIM), F32)],
        compiler_params=pltpu.CompilerParams(dimension_semantics=("arbitrary", "arbitrary"),
                                             vmem_limit_bytes=VMEM_LIMIT),
        name="delta_prompt",
    )(qkv, z, bg, ng, s0)


def _delta_sample_kernel(qkv_ref, z_ref, bg_ref, ng_ref, s0_ref, o_ref, snew_ref,
                         w_scr, qd_scr, kd_scr, u_scr, vn_scr, os_scr, *, T, rs):
    R = T * rs
    shift = rs.bit_length() - 1
    bg = bg_ref[...]
    ri = lax.broadcasted_iota(jnp.int32, (R, R), 0)
    ci = lax.broadcasted_iota(jnp.int32, (R, R), 1)
    same_seq = (ri & (rs - 1)) == (ci & (rs - 1))
    ti = lax.shift_right_logical(ri, shift)
    tj = lax.shift_right_logical(ci, shift)
    causal = same_seq & (ti >= tj)
    strict = same_seq & (ti > tj)
    eye = (ri == ci).astype(F32)
    gc_all = _mm_exact_lhs(jnp.where(causal, 1.0, 0.0).astype(BF16), bg)
    gc_t = gc_all.T
    scale = DN_HEAD_DIM ** -0.5
    seq_rows = [pl.ds(s, T, stride=rs) for s in range(rs)]

    def same_steps(s):
        block = s.bit_length() - 1
        return lax.shift_right_logical(ti, block) == lax.shift_right_logical(tj, block)

    for h in range(DN_HEADS):
        lo = h * DN_HEAD_DIM
        q = _dn_activation(qkv_ref[:, lo:lo + DN_HEAD_DIM], True) * scale
        k = _dn_activation(qkv_ref[:, DN_WIDTH + lo:DN_WIDTH + lo + DN_HEAD_DIM], True)
        v = _dn_activation(qkv_ref[:, 2 * DN_WIDTH + lo:2 * DN_WIDTH + lo + DN_HEAD_DIM], False)
        beta = bg[:, h:h + 1]
        gcc = gc_all[:, DN_HEADS + h:DN_HEADS + h + 1]
        gcr = gc_t[DN_HEADS + h:DN_HEADS + h + 1, :]
        glast = gc_all[R - rs:R, DN_HEADS + h:DN_HEADS + h + 1]
        glast_rows = jnp.concatenate([glast] * T, axis=0)
        decay = jnp.exp(jnp.where(causal, gcc - gcr, -jnp.inf))
        kb = k * beta
        a = jnp.where(strict, _mm_nt(kb, k) * decay, 0.0)
        tinv = _unit_lower_inverse(a, eye, same_steps, T, _mm)
        eg = jnp.exp(gcc)
        uw = _mm(tinv, jnp.concatenate([v * beta, kb * eg], axis=1))
        u_scr[...] = uw[:, :DN_HEAD_DIM]
        w_scr[...] = uw[:, DN_HEAD_DIM:]
        qd_scr[...] = q * eg
        kd_scr[...] = k * jnp.exp(glast_rows - gcc)
        qk = jnp.where(causal, _mm_nt(q, k) * decay, 0.0)
        eglast = jnp.exp(glast)
        wq = jnp.stack([jnp.concatenate([w_scr[rows, :], qd_scr[rows, :]], axis=0) for rows in seq_rows])
        us = jnp.stack([u_scr[rows, :] for rows in seq_rows])
        kds = jnp.stack([kd_scr[rows, :] for rows in seq_rows])
        egl = jnp.stack([eglast[s:s + 1, :] for s in range(rs)])
        state = s0_ref[:, h]
        ws = _bmm(wq, state)
        v_new = us - ws[:, :T]
        snew_ref[:, h] = state * egl + _bmm_tn(kds, v_new)
        for s, rows in enumerate(seq_rows):
            vn_scr[rows, :] = v_new[s]
            os_scr[rows, :] = ws[s, T:]
        o = os_scr[...] + _mm(qk, vn_scr[...])
        o_ref[:, lo:lo + DN_HEAD_DIM] = _head_norm_gate(o, z_ref[:, lo:lo + DN_HEAD_DIM], ng_ref[...])


def _delta_sample(qkv, z, bg, ng, s0, *, T, rs):
    G, R, _ = qkv.shape

    def tile(c):
        return pl.BlockSpec((None, R, c), lambda gi: (gi, 0, 0))

    s_spec = pl.BlockSpec((rs, DN_HEADS, DN_HEAD_DIM, DN_HEAD_DIM), lambda gi: (gi, 0, 0, 0))
    return pl.pallas_call(
        functools.partial(_delta_sample_kernel, T=T, rs=rs),
        grid=(G,),
        in_specs=[tile(3 * DN_WIDTH), tile(DN_WIDTH), tile(LANES),
                  pl.BlockSpec(ng.shape, lambda gi: (0, 0)), s_spec],
        out_specs=[tile(DN_WIDTH), s_spec],
        out_shape=[jax.ShapeDtypeStruct((G, R, DN_WIDTH), BF16), jax.ShapeDtypeStruct(s0.shape, F32)],
        scratch_shapes=[pltpu.VMEM((R, DN_HEAD_DIM), F32) for _ in range(6)],
        compiler_params=pltpu.CompilerParams(dimension_semantics=("arbitrary",),
                                             vmem_limit_bytes=VMEM_LIMIT),
        name="delta_sample",
    )(qkv, z, bg, ng, s0)


def _postmix_kernel(*refs, R, rs, NT):
    x_ref, conf_ref, o_ref, p_ref, fbuf0_ref = refs[:5]
    wout_ref, gffn_ref, wup_ref, fw_ref, fb_ref, wdown_ref, wple_ref, wpg_ref, gfin_ref = refs[5:5 + N_POST_W]
    y_ref, fbuf_ref, fext, act_scr = refs[5 + N_POST_W:]
    t = pl.program_id(1)
    _, HF = _halo(FFN_CONV_K, rs)
    _halo_begin(fext, fbuf0_ref, FFN_CONV_K, rs, R, t)

    mix = jnp.concatenate([conf_ref[...].astype(BF16), o_ref[...].astype(BF16)], axis=1)
    x1 = x_ref[...] + jnp.dot(mix, wout_ref[...], preferred_element_type=F32)
    hn = _rms(x1, gffn_ref[...]).astype(BF16)
    for j in range(2 * N_FF_CHUNKS):
        c0 = j * FF_COLS
        fext[HF:HF + R, c0:c0 + FF_COLS] = jnp.dot(hn, wup_ref[:, c0:c0 + FF_COLS],
                                                   preferred_element_type=F32)

    RC = 64
    for j in range(N_FF_CHUNKS):
        for r0 in range(0, R, RC):
            halves = []
            for c0 in (j * FF_COLS, D_FF + j * FF_COLS):
                cols = slice(c0, c0 + FF_COLS)
                halves.append(_causal_conv(fext, fw_ref, cols, FFN_CONV_K, rs, r0, RC) + fb_ref[:, cols])
            act_scr[r0:r0 + RC, j * FF_COLS:(j + 1) * FF_COLS] = (_silu(halves[0]) * halves[1]).astype(BF16)

    RB = 128
    for r0 in range(0, R, RB):
        rows = slice(r0, r0 + RB)
        x2 = x1[rows] + jnp.dot(act_scr[rows, :], wdown_ref[...], preferred_element_type=F32)
        ple = jnp.dot(p_ref[rows, :].astype(BF16), wple_ref[...], preferred_element_type=F32)
        x3 = x2 + ple * _sigmoid(jnp.dot(x2.astype(BF16), wpg_ref[...], preferred_element_type=F32))
        y_ref[rows, :] = _rms(x3, gfin_ref[...])

    _halo_end(fext, fbuf_ref, FFN_CONV_K, rs, R, t, NT)


def _postmix(x, conf, o, p, fbuf0, post_w, *, R, rs):
    G, L, _ = x.shape
    NT = L // R
    halo, HF = _halo(FFN_CONV_K, rs)

    def tile(c):
        return pl.BlockSpec((None, R, c), lambda gi, ti: (gi, ti, 0))

    def full(a):
        return pl.BlockSpec(a.shape, lambda gi, ti: (0,) * a.ndim)

    buf_spec = pl.BlockSpec((None, halo, 2 * D_FF), lambda gi, ti: (gi, 0, 0))
    return pl.pallas_call(
        functools.partial(_postmix_kernel, R=R, rs=rs, NT=NT),
        grid=(G, NT),
        in_specs=[tile(D_MODEL), tile(CONF_WIDTH), tile(DN_WIDTH), tile(D_PLE), buf_spec]
        + [full(a) for a in post_w],
        out_specs=[tile(D_MODEL), buf_spec],
        out_shape=[jax.ShapeDtypeStruct((G, L, D_MODEL), F32),
                   jax.ShapeDtypeStruct((G, halo, 2 * D_FF), F32)],
        scratch_shapes=[pltpu.VMEM((HF + R, 2 * D_FF), F32), pltpu.VMEM((R, D_FF), BF16)],
        compiler_params=pltpu.CompilerParams(dimension_semantics=("arbitrary", "arbitrary"),
                                             vmem_limit_bytes=VMEM_LIMIT),
        name="postmix",
    )(x, conf, o, p, fbuf0, *post_w)


def _sublane_replicated(w):
    return jnp.broadcast_to(w[:, None, :], (w.shape[0], SUBLANES, w.shape[1]))


def _interleave(a, rs):
    n, t, c = a.shape
    return a.reshape(n // rs, rs, t, c).transpose(0, 2, 1, 3).reshape(n // rs, t * rs, c)


def _deinterleave(a, rs, t):
    g, _, c = a.shape
    return a.reshape(g, t, rs, c).transpose(0, 2, 1, 3).reshape(g * rs, t, c)


def _trunk(x, p, conf_buf, dn_buf, s0, ffn_buf, pre_w, ng, post_w, *, interleaved):
    seq = x.shape[1]
    if interleaved:
        rs = SAMPLE_RS
        x, p, conf_buf, dn_buf, ffn_buf = (_interleave(a, rs) for a in (x, p, conf_buf, dn_buf, ffn_buf))
        pre_rows = post_rows = seq * rs
    else:
        rs = 1
        pre_rows, post_rows = PREMIX_ROWS, POSTMIX_ROWS
    conf, qkv, z, bg, conf_new, dn_new = _premix(x, conf_buf, dn_buf, pre_w, R=pre_rows, rs=rs)
    if interleaved:
        o, s_new = _delta_sample(qkv, z, bg, ng, s0, T=seq, rs=rs)
    else:
        o, s_new = _delta_prompt(qkv, z, bg, ng, s0, SB=DELTA_SEQS, R=DELTA_ROWS)
    y, ffn_new = _postmix(x, conf, o, p, ffn_buf, post_w, R=post_rows, rs=rs)
    if interleaved:
        y = _deinterleave(y, rs, seq)
        conf_new = _deinterleave(conf_new, rs, CONF_K - 1)
        dn_new = _deinterleave(dn_new, rs, DN_CONV_K - 1)
        ffn_new = _deinterleave(ffn_new, rs, FFN_CONV_K - 1)
    return y, conf_new[None], dn_new[None], s_new[None], ffn_new[None]


def kernel(x_prompt, x_sample, p_prompt, p_sample, state_conf_buf, state_dn_conv_buf, state_dn_S, state_ffn_buf,
           norm_mix_g, w_in, conf_dw_w, conf_dw_b, conf_ln_g, conf_ln_b, dn_conv_w, dn_a_log, dn_dt_bias,
           dn_norm_g, w_out, norm_ffn_g, w_up, ffn_conv_w, ffn_conv_b, w_down, w_ple, w_ple_gate, norm_final_g):
    assert w_in.shape[0] == 1, "single trunk layer"
    gate_pad = ((0, 0), (DN_HEADS, LANES - 2 * DN_HEADS))
    pre_w = (norm_mix_g[0][None], w_in[0].astype(BF16),
             jnp.pad(w_in[0][:, N_MAIN:], ((0, 0), (0, LANES - 2 * DN_HEADS))).astype(BF16),
             _sublane_replicated(conf_dw_w[0]), conf_dw_b[0][None], conf_ln_g[0][None], conf_ln_b[0][None],
             _sublane_replicated(dn_conv_w[0]), jnp.pad(dn_a_log[0][None], gate_pad),
             jnp.pad(dn_dt_bias[0][None], gate_pad))
    ng = dn_norm_g[0][None]
    post_w = (w_out[0].astype(BF16), norm_ffn_g[0][None], w_up[0].astype(BF16),
              _sublane_replicated(ffn_conv_w[0]), ffn_conv_b[0][None], w_down[0].astype(BF16),
              w_ple[0].astype(BF16), w_ple_gate[0].astype(BF16), norm_final_g[None])
    assert len(pre_w) == N_PRE_W and len(post_w) == N_POST_W

    bp, dt = x_prompt.shape[0], x_prompt.dtype
    z_conf = jnp.zeros((bp, CONF_K - 1, CONF_WIDTH), dt)
    z_dn = jnp.zeros((bp, DN_CONV_K - 1, 3 * DN_WIDTH), dt)
    z_s = jnp.zeros((bp, DN_HEADS, DN_HEAD_DIM, DN_HEAD_DIM), dt)
    z_ffn = jnp.zeros((bp, FFN_CONV_K - 1, 2 * D_FF), dt)
    yp, pc, pd, ps, pf = _trunk(x_prompt, p_prompt[0], z_conf, z_dn, z_s, z_ffn, pre_w, ng, post_w,
                                interleaved=False)
    ys, sc, sd, ss, sf = _trunk(x_sample, p_sample[0], state_conf_buf[0], state_dn_conv_buf[0], state_dn_S[0],
                                state_ffn_buf[0], pre_w, ng, post_w, interleaved=True)
    return (yp, ys, pc, pd, ps, pf, sc, sd, ss, sf)
```
